```python
import math
import jax, jax.numpy as jnp
from jax import lax
import numpy as np

D_MODEL = 1024
BATCH = 1
SEQ = 16384
DEPTH = 4

CHUNK = 64
N_A_LAYERS = DEPTH // 2
N_B_LAYERS = DEPTH - N_A_LAYERS

SSM_EXPAND = 2
D_INNER = SSM_EXPAND * D_MODEL
SSM_HEAD_DIM = 64
SSM_HEADS = D_INNER // SSM_HEAD_DIM
SSM_GROUPS = 8
SSM_HEADS_PER_GROUP = SSM_HEADS // SSM_GROUPS
D_STATE = 128
D_CONV = 4
CONV_DIM = D_INNER + 2 * SSM_GROUPS * D_STATE
D_IN_PROJ = 2 * D_INNER + 2 * SSM_GROUPS * D_STATE + SSM_HEADS
DT_MIN = 1e-3
DT_MAX = 1e-1

SB_HEADS = 16
SB_HEAD_DIM = D_MODEL // SB_HEADS
SB_WIDTH = SB_HEADS * SB_HEAD_DIM
Q_BLOCK = 128

D_FF = 4 * D_MODEL

DEEPNORM_ALPHA = (2 * DEPTH) ** 0.25
DEEPNORM_BETA = (8 * DEPTH) ** -0.25
LN_EPS = 1e-5
RMS_EPS = 1e-5

kernel_name = "ssd_stickbreak_yoco_deepnorm_trunk"


def layer_norm(x, g, b):
    xf = x.astype(jnp.float32)
    mu = jnp.mean(xf, axis=-1, keepdims=True)
    var = jnp.mean(jnp.square(xf - mu), axis=-1, keepdims=True)
    return ((xf - mu) * lax.rsqrt(var + LN_EPS) * g.astype(jnp.float32) + b.astype(jnp.float32)).astype(x.dtype)


def causal_depthwise_conv(u, w, b):
    c = u.shape[-1]
    y = lax.conv_general_dilated(
        u, w[:, None, :], window_strides=(1,), padding=[(D_CONV - 1, 0)],
        dimension_numbers=("NWC", "WIO", "NWC"), feature_group_count=c)
    return y + b


def segsum(a):
    t = a.shape[-1]
    cs = jnp.cumsum(a, axis=-1)
    diff = cs[..., :, None] - cs[..., None, :]
    mask = jnp.tril(jnp.ones((t, t), dtype=bool))
    return jnp.where(mask, diff, -jnp.inf)


def ssd_chunked_scan(x, da, bm, cm):
    b, l, h, p = x.shape
    c = l // CHUNK
    g, r, n = SSM_GROUPS, SSM_HEADS_PER_GROUP, D_STATE
    xc = x.reshape(b, c, CHUNK, g, r, p)
    bc = bm.reshape(b, c, CHUNK, g, n)
    cc = cm.reshape(b, c, CHUNK, g, n)
    a = da.astype(jnp.float32).reshape(b, c, CHUNK, g, r).transpose(0, 1, 3, 4, 2)
    a_cs = jnp.cumsum(a, axis=-1)
    decay_mat = jnp.exp(segsum(a))
    cb = jnp.einsum("bclgn,bcsgn->bcgls", cc, bc)
    y_diag = jnp.einsum("bcgls,bcgrls,bcsgrp->bclgrp", cb, decay_mat, xc)
    decay_states = jnp.exp(a_cs[..., -1:] - a_cs)
    states = jnp.einsum("bclgn,bcgrl,bclgrp->bcgrpn", bc, decay_states, xc).astype(jnp.float32)
    block_decay = jnp.exp(a_cs[..., -1])

    def step(carry, inp):
        s_c, d_c = inp
        return carry * d_c[..., None, None] + s_c, carry

    init = jnp.zeros((b, g, r, p, n), jnp.float32)
    _, prev = lax.scan(step, init, (jnp.moveaxis(states, 1, 0), jnp.moveaxis(block_decay, 1, 0)))
    prev = jnp.moveaxis(prev, 0, 1)
    y_off = jnp.einsum("bclgn,bcgrpn,bcgrl->bclgrp", cc, prev, jnp.exp(a_cs))
    return (y_diag + y_off).reshape(b, l, h, p).astype(x.dtype)


def mamba2_mixer(u, w_in, conv_w, conv_b, dt_bias, a_log, d_skip, norm_w, w_out):
    b, l, _ = u.shape
    zxbcdt = u @ w_in
    z = zxbcdt[..., :D_INNER]
    xbc = zxbcdt[..., D_INNER:D_INNER + CONV_DIM]
    dt_raw = zxbcdt[..., D_INNER + CONV_DIM:]
    xbc = jax.nn.silu(causal_depthwise_conv(xbc, conv_w, conv_b))
    xs = xbc[..., :D_INNER]
    bm = xbc[..., D_INNER:D_INNER + SSM_GROUPS * D_STATE].reshape(b, l, SSM_GROUPS, D_STATE)
    cm = xbc[..., D_INNER + SSM_GROUPS * D_STATE:].reshape(b, l, SSM_GROUPS, D_STATE)
    dt = jax.nn.softplus((dt_raw + dt_bias).astype(jnp.float32))
    a = -jnp.exp(a_log.astype(jnp.float32))
    xh = xs.reshape(b, l, SSM_HEADS, SSM_HEAD_DIM)
    y = ssd_chunked_scan(xh * dt[..., None].astype(xh.dtype), dt * a, bm, cm)
    y = (y + xh * d_skip[:, None]).reshape(b, l, D_INNER)
    yg = (y * jax.nn.silu(z)).astype(jnp.float32).reshape(b, l, SSM_GROUPS, D_INNER // SSM_GROUPS)
    yg = yg * lax.rsqrt(jnp.mean(jnp.square(yg), axis=-1, keepdims=True) + RMS_EPS)
    y = (yg.reshape(b, l, D_INNER) * norm_w.astype(jnp.float32)).astype(u.dtype)
    return y @ w_out


def shared_kv(h, w_k, w_v):
    b, l, _ = h.shape
    k = (h @ w_k).reshape(b, l, SB_HEADS, SB_HEAD_DIM).transpose(0, 2, 1, 3)
    v = (h @ w_v).reshape(b, l, SB_HEADS, SB_HEAD_DIM).transpose(0, 2, 1, 3)
    return k, v


def stick_breaking_attention(u, w_q, k, v, w_o):
    b, l, _ = u.shape
    q = (u @ w_q).reshape(b, l, SB_HEADS, SB_HEAD_DIM).transpose(0, 2, 1, 3) * (SB_HEAD_DIM ** -0.5)
    nb = l // Q_BLOCK
    kb = jnp.moveaxis(k.reshape(b, SB_HEADS, nb, Q_BLOCK, SB_HEAD_DIM), 2, 0)
    vb = jnp.moveaxis(v.reshape(b, SB_HEADS, nb, Q_BLOCK, SB_HEAD_DIM), 2, 0)
    idx = jnp.arange(Q_BLOCK)
    diag_mask = idx[None, :] < idx[:, None]
    later = (idx[:, None] > idx[None, :]).astype(jnp.float32)

    outs = []
    for i in range(nb):
        qi = q[:, :, i * Q_BLOCK:(i + 1) * Q_BLOCK]

        def step(carry, inp, qi=qi):
            acc, o = carry
            kj, vj, is_diag = inp
            z = jnp.einsum("bhqd,bhkd->bhqk", qi, kj).astype(jnp.float32)
            valid = jnp.logical_or(jnp.logical_not(is_diag), diag_mask)
            lo = jnp.where(valid, jax.nn.log_sigmoid(-z), 0.0)
            after = jnp.einsum("bhqj,js->bhqs", lo, later) + acc[..., None]
            logw = jnp.where(valid, z + lo + after, -jnp.inf)
            o = o + jnp.einsum("bhqk,bhkd->bhqd", jnp.exp(logw).astype(vj.dtype), vj).astype(jnp.float32)
            acc = acc + jnp.sum(lo, axis=-1)
            return (acc, o), None

        init = (jnp.zeros((b, SB_HEADS, Q_BLOCK), jnp.float32),
                jnp.zeros((b, SB_HEADS, Q_BLOCK, SB_HEAD_DIM), jnp.float32))
        xs = (kb[:i + 1][::-1], vb[:i + 1][::-1], jnp.arange(i + 1) == 0)
        (_, o_i), _ = lax.scan(step, init, xs)
        outs.append(o_i.astype(u.dtype))
    o = jnp.concatenate(outs, axis=2)
    o = o.transpose(0, 2, 1, 3).reshape(b, l, SB_WIDTH)
    return o @ w_o


def squared_relu_mlp(h, w1, w2):
    return jnp.square(jax.nn.relu(h @ w1)) @ w2


def setup_inputs(seed: int = 0) -> dict:
    key = jax.random.key(seed)
    ks = jax.random.split(key, 20)
    f32 = jnp.float32

    def nrm(k, shape, scale):
        return jax.random.normal(k, shape, f32) * scale

    x = nrm(ks[0], (BATCH, SEQ, D_MODEL), 1.0)
    ssm_w_in = nrm(ks[1], (N_A_LAYERS, D_MODEL, D_IN_PROJ), D_MODEL ** -0.5)
    ssm_conv_w = jax.random.uniform(ks[2], (N_A_LAYERS, D_CONV, CONV_DIM), f32, -0.5, 0.5)
    ssm_conv_b = nrm(ks[3], (N_A_LAYERS, CONV_DIM), 0.02)
    dt = jnp.exp(jax.random.uniform(ks[4], (N_A_LAYERS, SSM_HEADS), f32)
                 * (math.log(DT_MAX) - math.log(DT_MIN)) + math.log(DT_MIN))
    dt = jnp.maximum(dt, 1e-4)
    ssm_dt_bias = dt + jnp.log(-jnp.expm1(-dt))
    ssm_a_log = jnp.log(jax.random.uniform(ks[5], (N_A_LAYERS, SSM_HEADS), f32, 1.0, 16.0))
    ssm_d = 1.0 + nrm(ks[6], (N_A_LAYERS, SSM_HEADS), 0.1)
    ssm_norm_w = 1.0 + nrm(ks[7], (N_A_LAYERS, D_INNER), 0.1)
    ssm_w_out = nrm(ks[8], (N_A_LAYERS, D_INNER, D_MODEL), D_INNER ** -0.5 * DEEPNORM_BETA)
    sb_w_k = nrm(ks[9], (D_MODEL, SB_WIDTH), D_MODEL ** -0.5)
    sb_w_v = nrm(ks[10], (D_MODEL, SB_WIDTH), D_MODEL ** -0.5 * DEEPNORM_BETA)
    sb_w_q = nrm(ks[11], (N_B_LAYERS, D_MODEL, SB_WIDTH), D_MODEL ** -0.5)
    sb_w_o = nrm(ks[12], (N_B_LAYERS, SB_WIDTH, D_MODEL), SB_WIDTH ** -0.5 * DEEPNORM_BETA)
    mlp_w1 = nrm(ks[13], (DEPTH, D_MODEL, D_FF), D_MODEL ** -0.5)
    mlp_w2 = nrm(ks[14], (DEPTH, D_FF, D_MODEL), D_FF ** -0.5 * DEEPNORM_BETA)
    ln_mix_g = 1.0 + nrm(ks[15], (DEPTH, D_MODEL), 0.05)
    ln_mix_b = nrm(ks[16], (DEPTH, D_MODEL), 0.02)
    ln_mlp_g = 1.0 + nrm(ks[17], (DEPTH, D_MODEL), 0.05)
    ln_mlp_b = nrm(ks[18], (DEPTH, D_MODEL), 0.02)
    return {"x": x, "ssm_w_in": ssm_w_in, "ssm_conv_w": ssm_conv_w, "ssm_conv_b": ssm_conv_b,
            "ssm_dt_bias": ssm_dt_bias, "ssm_a_log": ssm_a_log, "ssm_d": ssm_d,
            "ssm_norm_w": ssm_norm_w, "ssm_w_out": ssm_w_out, "sb_w_k": sb_w_k, "sb_w_v": sb_w_v,
            "sb_w_q": sb_w_q, "sb_w_o": sb_w_o, "mlp_w1": mlp_w1, "mlp_w2": mlp_w2,
            "ln_mix_g": ln_mix_g, "ln_mix_b": ln_mix_b, "ln_mlp_g": ln_mlp_g, "ln_mlp_b": ln_mlp_b}


def reference(x, ssm_w_in, ssm_conv_w, ssm_conv_b, ssm_dt_bias, ssm_a_log, ssm_d, ssm_norm_w,
              ssm_w_out, sb_w_k, sb_w_v, sb_w_q, sb_w_o, mlp_w1, mlp_w2,
              ln_mix_g, ln_mix_b, ln_mlp_g, ln_mlp_b):
    h = x
    k = v = None
    for layer in range(DEPTH):
        if layer < N_A_LAYERS:
            i = layer
            mix = mamba2_mixer(h, ssm_w_in[i], ssm_conv_w[i], ssm_conv_b[i], ssm_dt_bias[i],
                               ssm_a_log[i], ssm_d[i], ssm_norm_w[i], ssm_w_out[i])
        else:
            if layer == N_A_LAYERS:
                k, v = shared_kv(h, sb_w_k, sb_w_v)
            j = layer - N_A_LAYERS
            mix = stick_breaking_attention(h, sb_w_q[j], k, v, sb_w_o[j])
        h = layer_norm(DEEPNORM_ALPHA * h + mix, ln_mix_g[layer], ln_mix_b[layer])
        h = layer_norm(DEEPNORM_ALPHA * h + squared_relu_mlp(h, mlp_w1[layer], mlp_w2[layer]),
                       ln_mlp_g[layer], ln_mlp_b[layer])
    return h
```

```python
import functools
import math

import jax
import jax.numpy as jnp
from jax import lax
from jax.experimental import pallas as pl
from jax.experimental.pallas import tpu as pltpu

F32 = jnp.float32
BF16 = jnp.bfloat16

D_MODEL = 1024
DEPTH = 4
N_SSD_LAYERS = DEPTH // 2
D_INNER = 2 * D_MODEL
SSM_HEADS = 32
SSM_HEAD_DIM = 64
SSM_GROUPS = 8
D_STATE = 128
D_CONV = 4
CONV_DIM = D_INNER + 2 * SSM_GROUPS * D_STATE
SB_HEADS = 16
SB_HEAD_DIM = 64
D_FF = 4 * D_MODEL
DEEPNORM_ALPHA = (2 * DEPTH) ** 0.25
LN_EPS = 1e-5
RMS_EPS = 1e-5

LANES = 128
SUBLANES = 8
HEADS_PER_VREG = LANES // SSM_HEAD_DIM
SSD_CHUNK = 128
ATT_BLOCK = 128
LOG_F32_UNDERFLOW = -104.0
VMEM_LIMIT_CAP = 58 * 1024 * 1024

NT_DIMS = (((1,), (1,)), ((), ()))
TN_DIMS = (((0,), (0,)), ((), ()))


def _softplus(x):
    return jnp.maximum(x, 0.0) + jnp.log1p(jnp.exp(-jnp.abs(x)))


def _silu(x):
    return x * (1.0 / (1.0 + jnp.exp(-x)))


def _layer_norm(x, g, b):
    mu = jnp.mean(x, axis=-1, keepdims=True)
    xc = x - mu
    var = jnp.mean(xc * xc, axis=-1, keepdims=True)
    return xc * lax.rsqrt(var + LN_EPS) * g + b


def _split_bf16(x, parts):
    out = []
    r = x
    for _ in range(parts - 1):
        p = r.astype(BF16)
        out.append(p)
        r = r - p.astype(F32)
    out.append(r.astype(BF16))
    return out


def _dot01_left(m01, x, parts=3):
    acc = None
    for p in _split_bf16(x, parts):
        t = jnp.dot(m01, p, preferred_element_type=F32)
        acc = t if acc is None else acc + t
    return acc


def _dot01_right(x, m01, parts=3):
    acc = None
    for p in _split_bf16(x, parts):
        t = jnp.dot(p, m01, preferred_element_type=F32)
        acc = t if acc is None else acc + t
    return acc


def _resident(shape):
    nd = len(shape)
    return pl.BlockSpec(shape, lambda i: (0,) * nd, pipeline_mode=pl.Buffered(1))


def _rows(tm, width):
    return pl.BlockSpec((tm, width), lambda i: (i, 0))


def _compiler_params(vmem_bytes, semantics=("arbitrary",)):
    return pltpu.CompilerParams(dimension_semantics=semantics,
                                vmem_limit_bytes=min(int(vmem_bytes), VMEM_LIMIT_CAP))


def _ssd_in_kernel(h_ref, wz_ref, wx_ref, wdt_ref, wdtT_ref, cw_ref, cb_ref, dtb_ref, dtbT_ref,
                   z_ref, xs_ref, b_ref, c_ref, dt_ref, dtT_ref, carry_ref, work_ref, *, tm, cchunk):
    hb = h_ref[...].astype(BF16)
    z_ref[...] = jnp.dot(hb, wz_ref[...], preferred_element_type=F32)
    dt_ref[...] = _softplus(jnp.dot(hb, wdt_ref[...], preferred_element_type=F32) + dtb_ref[...])
    dtT_ref[...] = _softplus(
        lax.dot_general(wdtT_ref[...], hb, NT_DIMS, preferred_element_type=F32) + dtbT_ref[...])

    @pl.when(pl.program_id(0) == 0)
    def _():
        carry_ref[...] = jnp.zeros_like(carry_ref)

    for c in range(CONV_DIM // cchunk):
        lo = c * cchunk
        cols = slice(lo, lo + cchunk)
        work_ref[0:SUBLANES, :] = carry_ref[:, cols]
        work_ref[SUBLANES:SUBLANES + tm, :] = jnp.dot(hb, wx_ref[:, cols], preferred_element_type=F32)
        carry_ref[:, cols] = work_ref[tm:tm + SUBLANES, :]
        acc = cb_ref[:, cols]
        for k in range(D_CONV):
            start = SUBLANES - (D_CONV - 1) + k
            acc = acc + cw_ref[k:k + 1, cols] * work_ref[start:start + tm, :]
        y = _silu(acc)
        if lo < D_INNER:
            xs_ref[:, cols] = y
        elif lo < D_INNER + SSM_GROUPS * D_STATE:
            b_ref[:, lo - D_INNER:lo - D_INNER + cchunk] = y.astype(BF16)
        else:
            off = lo - D_INNER - SSM_GROUPS * D_STATE
            c_ref[:, off:off + cchunk] = y.astype(BF16)


def _ssd_in(h, wz, wx, wdt, wdtT, conv_w, conv_b, dtb, dtbT, *, tm=256, cchunk=512):
    L = h.shape[0]
    gn = SSM_GROUPS * D_STATE
    vmem = (2 * (D_MODEL * (D_INNER + CONV_DIM + LANES)) + 2 * tm * D_MODEL * 4
            + 2 * tm * (2 * D_INNER * 4 + 2 * gn * 2 + LANES * 4 + SSM_HEADS * 4)
            + (tm + SUBLANES) * cchunk * 4 + 6 * tm * cchunk * 4 + (8 << 20))
    return pl.pallas_call(
        functools.partial(_ssd_in_kernel, tm=tm, cchunk=cchunk),
        grid=(L // tm,),
        in_specs=[_rows(tm, D_MODEL), _resident(wz.shape), _resident(wx.shape), _resident(wdt.shape),
                  _resident(wdtT.shape), _resident(conv_w.shape), _resident(conv_b.shape),
                  _resident(dtb.shape), _resident(dtbT.shape)],
        out_specs=[_rows(tm, D_INNER), _rows(tm, D_INNER), _rows(tm, gn), _rows(tm, gn),
                   _rows(tm, LANES), pl.BlockSpec((SSM_HEADS, tm), lambda i: (0, i))],
        out_shape=[jax.ShapeDtypeStruct((L, D_INNER), F32), jax.ShapeDtypeStruct((L, D_INNER), F32),
                   jax.ShapeDtypeStruct((L, gn), BF16), jax.ShapeDtypeStruct((L, gn), BF16),
                   jax.ShapeDtypeStruct((L, LANES), F32), jax.ShapeDtypeStruct((SSM_HEADS, L), F32)],
        scratch_shapes=[pltpu.VMEM((SUBLANES, CONV_DIM), F32), pltpu.VMEM((tm + SUBLANES, cchunk), F32)],
        compiler_params=_compiler_params(vmem),
        name="ssd_in",
    )(h, wz, wx, wdt, wdtT, conv_w, conv_b, dtb, dtbT)


def _ssd_scan_kernel(xs_ref, z_ref, b_ref, c_ref, dt_ref, dtT_ref, alog_ref, alogT_ref, dskip_ref,
                     nw_ref, y_ref, state_ref):
    q = SSD_CHUNK

    @pl.when(pl.program_id(0) == 0)
    def _():
        state_ref[...] = jnp.zeros_like(state_ref)

    dt = dt_ref[...]
    dtT = dtT_ref[...]
    da = dt * (-jnp.exp(alog_ref[...]))
    daT = dtT * (-jnp.exp(alogT_ref[...]))
    ri = lax.broadcasted_iota(jnp.int32, (q, q), 0)
    ci = lax.broadcasted_iota(jnp.int32, (q, q), 1)
    tril = ci <= ri
    ltri = jnp.where(tril, 1.0, 0.0).astype(BF16)
    utri = jnp.where(ri <= ci, 1.0, 0.0).astype(BF16)
    cs = _dot01_left(ltri, da)
    csT = _dot01_right(daT, utri)
    cs_last = cs[q - 1:q, :]
    ecs = jnp.exp(cs)
    wdt = jnp.exp(cs_last - cs) * dt
    edec = jnp.exp(cs_last)
    first = lax.broadcasted_iota(jnp.int32, (1, LANES), 1) < SSM_HEAD_DIM

    def per_head(v, h0):
        return jnp.where(first, v[:, h0:h0 + 1], v[:, h0 + 1:h0 + 2])

    for g in range(SSM_GROUPS):
        gcols = slice(g * D_STATE, (g + 1) * D_STATE)
        bg = b_ref[:, gcols]
        cg = c_ref[:, gcols]
        cb = lax.dot_general(cg, bg, NT_DIMS, preferred_element_type=F32)
        gated = []
        for pr in range(2):
            pair = 2 * g + pr
            h0 = HEADS_PER_VREG * pair
            cols = slice(pair * LANES, (pair + 1) * LANES)
            x = xs_ref[:, cols]
            xb = x.astype(BF16)
            halves = []
            for h in (h0, h0 + 1):
                diff = cs[:, h:h + 1] - csT[h:h + 1, :]
                dec = jnp.exp(jnp.where(tril, diff, -jnp.inf))
                m = (cb * dec * dtT[h:h + 1, :]).astype(BF16)
                halves.append(jnp.dot(m, xb, preferred_element_type=F32))
            y = jnp.where(first, halves[0], halves[1])
            st = state_ref[pair]
            y = y + jnp.dot(cg, st.astype(BF16), preferred_element_type=F32) * per_head(ecs, h0)
            xw = (x * per_head(wdt, h0)).astype(BF16)
            upd = lax.dot_general(bg, xw, TN_DIMS, preferred_element_type=F32)
            state_ref[pair] = st * per_head(edec, h0) + upd
            y = y + x * dskip_ref[:, cols]
            gated.append(y * _silu(z_ref[:, cols]))
        ss = (jnp.sum(gated[0] * gated[0], axis=-1, keepdims=True)
              + jnp.sum(gated[1] * gated[1], axis=-1, keepdims=True))
        scale = lax.rsqrt(ss * (1.0 / (2 * LANES)) + RMS_EPS)
        for pr in range(2):
            cols = slice((2 * g + pr) * LANES, (2 * g + pr + 1) * LANES)
            y_ref[:, cols] = (gated[pr] * scale * nw_ref[:, cols]).astype(BF16)


def _ssd_scan(xs, z, bm, cm, dt, dtT, alog, alogT, dskip, nw):
    L = xs.shape[0]
    q = SSD_CHUNK
    gn = SSM_GROUPS * D_STATE
    n_pairs = SSM_HEADS // HEADS_PER_VREG
    return pl.pallas_call(
        _ssd_scan_kernel,
        grid=(L // q,),
        in_specs=[_rows(q, D_INNER), _rows(q, D_INNER), _rows(q, gn), _rows(q, gn), _rows(q, LANES),
                  pl.BlockSpec((SSM_HEADS, q), lambda i: (0, i)), _resident(alog.shape),
                  _resident(alogT.shape), _resident(dskip.shape), _resident(nw.shape)],
        out_specs=_rows(q, D_INNER),
        out_shape=jax.ShapeDtypeStruct((L, D_INNER), BF16),
        scratch_shapes=[pltpu.VMEM((n_pairs, D_STATE, LANES), F32)],
        compiler_params=_compiler_params(32 << 20),
        name="ssd_scan",
    )(xs, z, bm, cm, dt, dtT, alog, alogT, dskip, nw)


def _post_kernel(m_ref, h_ref, wp_ref, g1_ref, b1_ref, w1_ref, w2_ref, g2_ref, b2_ref, out_ref, *, ff_chunk):
    mix = jnp.dot(m_ref[...], wp_ref[...], preferred_element_type=F32)
    h1 = _layer_norm(DEEPNORM_ALPHA * h_ref[...] + mix, g1_ref[...], b1_ref[...])
    hb = h1.astype(BF16)
    acc = None
    for c in range(D_FF // ff_chunk):
        cols = slice(c * ff_chunk, (c + 1) * ff_chunk)
        a = jnp.maximum(jnp.dot(hb, w1_ref[:, cols], preferred_element_type=F32), 0.0)
        t = jnp.dot((a * a).astype(BF16), w2_ref[cols, :], preferred_element_type=F32)
        acc = t if acc is None else acc + t
    out_ref[...] = _layer_norm(DEEPNORM_ALPHA * h1 + acc, g2_ref[...], b2_ref[...])


def _post(m, h, wp, g1, b1, w1, w2, g2, b2, *, tm=256, ff_chunk=1024):
    L, kdim = m.shape
    vmem = (2 * (kdim * D_MODEL + 2 * D_MODEL * D_FF) + 2 * tm * (kdim * 2 + 2 * D_MODEL * 4)
            + 6 * tm * max(ff_chunk, D_MODEL) * 4 + (8 << 20))
    return pl.pallas_call(
        functools.partial(_post_kernel, ff_chunk=ff_chunk),
        grid=(L // tm,),
        in_specs=[_rows(tm, kdim), _rows(tm, D_MODEL), _resident(wp.shape), _resident(g1.shape),
                  _resident(b1.shape), _resident(w1.shape), _resident(w2.shape), _resident(g2.shape),
                  _resident(b2.shape)],
        out_specs=_rows(tm, D_MODEL),
        out_shape=jax.ShapeDtypeStruct((L, D_MODEL), F32),
        compiler_params=_compiler_params(vmem),
        name="post",
    )(m, h, wp, g1, b1, w1, w2, g2, b2)


def _kv_kernel(h_ref, wk_ref, wv_ref, k_ref, v_ref):
    hb = h_ref[...].astype(BF16)
    k_ref[...] = jnp.dot(hb, wk_ref[...], preferred_element_type=F32).astype(BF16)
    v_ref[...] = jnp.dot(hb, wv_ref[...], preferred_element_type=F32).astype(BF16)


def _kv(h, wk, wv, *, tm=512):
    L = h.shape[0]
    vmem = 2 * 2 * D_MODEL * D_MODEL + 2 * tm * D_MODEL * (4 + 2 + 2) + 4 * tm * D_MODEL * 4 + (8 << 20)
    return pl.pallas_call(
        _kv_kernel,
        grid=(L // tm,),
        in_specs=[_rows(tm, D_MODEL), _resident(wk.shape), _resident(wv.shape)],
        out_specs=[_rows(tm, D_MODEL), _rows(tm, D_MODEL)],
        out_shape=[jax.ShapeDtypeStruct((L, D_MODEL), BF16), jax.ShapeDtypeStruct((L, D_MODEL), BF16)],
        compiler_params=_compiler_params(vmem),
        name="kv",
    )(h, wk, wv)


def _attn_kernel(h_ref, wq_ref, k_hbm, v_hbm, o_ref, kbuf, vbuf, sem, qa_ref, qb_ref, acc_ref, o_acc_ref):
    tq = ATT_BLOCK
    i = pl.program_id(0)

    def copies(j, slot):
        rows = pl.ds(pl.multiple_of(j * tq, tq), tq)
        return (pltpu.make_async_copy(k_hbm.at[rows, :], kbuf.at[slot], sem.at[0, slot]),
                pltpu.make_async_copy(v_hbm.at[rows, :], vbuf.at[slot], sem.at[1, slot]))

    def fetch(j, slot):
        for cp in copies(j, slot):
            cp.start()

    def wait(j, slot):
        for cp in copies(j, slot):
            cp.wait()

    fetch(i, 0)

    lane = lax.broadcasted_iota(jnp.int32, (1, LANES), 1)
    first = lane < SB_HEAD_DIM
    first_all = (lax.broadcasted_iota(jnp.int32, (1, D_MODEL), 1) % LANES) < SB_HEAD_DIM
    q = jnp.dot(h_ref[...].astype(BF16), wq_ref[...], preferred_element_type=F32) * (SB_HEAD_DIM ** -0.5)
    qa_ref[...] = jnp.where(first_all, q, 0.0).astype(BF16)
    qb_ref[...] = jnp.where(first_all, 0.0, q).astype(BF16)
    acc_ref[...] = jnp.where(lane < SB_HEADS, 0.0, -jnp.inf) + jnp.zeros((tq, LANES), F32)
    o_acc_ref[...] = jnp.zeros_like(o_acc_ref)

    ri = lax.broadcasted_iota(jnp.int32, (tq, tq), 0)
    ci = lax.broadcasted_iota(jnp.int32, (tq, tq), 1)
    later = jnp.where(ri > ci, 1.0, 0.0).astype(BF16)
    strictly_before = ci < ri

    def process(slot, diag):
        for pair in range(SB_HEADS // HEADS_PER_VREG):
            cols = slice(pair * LANES, (pair + 1) * LANES)
            kp = kbuf[slot, :, cols]
            vp = vbuf[slot, :, cols]
            outs = []
            for half, q_ref in enumerate((qa_ref, qb_ref)):
                hd = HEADS_PER_VREG * pair + half
                z = lax.dot_general(q_ref[:, cols], kp, NT_DIMS, preferred_element_type=F32)
                lo = -_softplus(z)
                if diag:
                    lo = jnp.where(strictly_before, lo, 0.0)
                after = _dot01_right(lo, later, parts=2)
                acc_h = acc_ref[:, hd:hd + 1]
                w = jnp.exp(z + lo + after + acc_h)
                if diag:
                    w = jnp.where(strictly_before, w, 0.0)
                outs.append(jnp.dot(w.astype(BF16), vp, preferred_element_type=F32))
                acc_ref[:, hd:hd + 1] = acc_h + after[:, 0:1] + lo[:, 0:1]
            o_acc_ref[:, cols] += jnp.where(first, outs[0], outs[1])

    def keep_going():
        return (jnp.max(acc_ref[...]) >= LOG_F32_UNDERFLOW).astype(jnp.int32)

    @pl.when(i > 0)
    def _():
        fetch(i - 1, 1)

    wait(i, 0)
    process(0, True)

    def cond(carry):
        j, _, go = carry
        return jnp.logical_and(j >= 0, go > 0)

    def body(carry):
        j, slot, _ = carry
        wait(j, slot)

        @pl.when(j > 0)
        def _():
            fetch(j - 1, 1 - slot)

        process(slot, False)
        return j - 1, 1 - slot, keep_going()

    j_end, slot_end, _ = lax.while_loop(cond, body, (i - 1, jnp.int32(1), keep_going()))

    @pl.when(j_end >= 0)
    def _():
        wait(j_end, slot_end)

    o_ref[...] = o_acc_ref[...].astype(BF16)


def _attn(h, wq, k, v):
    L = h.shape[0]
    tq = ATT_BLOCK
    return pl.pallas_call(
        _attn_kernel,
        grid=(L // tq,),
        in_specs=[_rows(tq, D_MODEL), _resident(wq.shape), pl.BlockSpec(memory_space=pl.ANY),
                  pl.BlockSpec(memory_space=pl.ANY)],
        out_specs=_rows(tq, D_MODEL),
        out_shape=jax.ShapeDtypeStruct((L, D_MODEL), BF16),
        scratch_shapes=[pltpu.VMEM((2, tq, D_MODEL), BF16), pltpu.VMEM((2, tq, D_MODEL), BF16),
                        pltpu.SemaphoreType.DMA((2, 2)), pltpu.VMEM((tq, D_MODEL), BF16),
                        pltpu.VMEM((tq, D_MODEL), BF16), pltpu.VMEM((tq, LANES), F32),
                        pltpu.VMEM((tq, D_MODEL), F32)],
        compiler_params=_compiler_params(32 << 20),
        name="attn",
    )(h, wq, k, v)


def _row(v):
    return v.reshape(1, -1).astype(F32)


def _pad_lanes(v, width=LANES):
    return jnp.pad(v, [(0, 0)] * (v.ndim - 1) + [(0, width - v.shape[-1])])


def kernel(x, ssm_w_in, ssm_conv_w, ssm_conv_b, ssm_dt_bias, ssm_a_log, ssm_d, ssm_norm_w, ssm_w_out,
           sb_w_k, sb_w_v, sb_w_q, sb_w_o, mlp_w1, mlp_w2, ln_mix_g, ln_mix_b, ln_mlp_g, ln_mlp_b):
    assert x.shape[0] == 1 and x.shape[2] == D_MODEL
    L = x.shape[1]
    assert L % 512 == 0
    h = x.reshape(L, D_MODEL)
    k = v = None
    for layer in range(DEPTH):
        if layer < N_SSD_LAYERS:
            w_in = ssm_w_in[layer]
            wz = w_in[:, :D_INNER].astype(BF16)
            wx = w_in[:, D_INNER:D_INNER + CONV_DIM].astype(BF16)
            w_dt = w_in[:, D_INNER + CONV_DIM:]
            wdt = _pad_lanes(w_dt).astype(BF16)
            wdtT = w_dt.T.astype(BF16)
            dtb = _pad_lanes(_row(ssm_dt_bias[layer]))
            dtbT = ssm_dt_bias[layer].reshape(-1, 1).astype(F32)
            z, xs, bm, cm, dt, dtT = _ssd_in(h, wz, wx, wdt, wdtT, ssm_conv_w[layer],
                                             _row(ssm_conv_b[layer]), dtb, dtbT)
            alog = _pad_lanes(_row(ssm_a_log[layer]))
            alogT = ssm_a_log[layer].reshape(-1, 1).astype(F32)
            dskip = _row(jnp.repeat(ssm_d[layer], SSM_HEAD_DIM))
            mix_in = _ssd_scan(xs, z, bm, cm, dt, dtT, alog, alogT, dskip, _row(ssm_norm_w[layer]))
            w_proj = ssm_w_out[layer].astype(BF16)
        else:
            if layer == N_SSD_LAYERS:
                k, v = _kv(h, sb_w_k.astype(BF16), sb_w_v.astype(BF16))
            j = layer - N_SSD_LAYERS
            mix_in = _attn(h, sb_w_q[j].astype(BF16), k, v)
            w_proj = sb_w_o[j].astype(BF16)
        h = _post(mix_in, h, w_proj, _row(ln_mix_g[layer]), _row(ln_mix_b[layer]),
                  mlp_w1[layer].astype(BF16), mlp_w2[layer].astype(BF16),
                  _row(ln_mlp_g[layer]), _row(ln_mlp_b[layer]))
    return h.reshape(1, L, D_MODEL)
```

```python
import functools
import math

import jax
import jax.numpy as jnp
from jax import lax
from jax.experimental import pallas as pl
from jax.experimental.pallas import tpu as pltpu

F32 = jnp.float32
BF16 = jnp.bfloat16

D_MODEL = 1024
DEPTH = 4
N_SSD_LAYERS = DEPTH // 2
D_INNER = 2 * D_MODEL
SSM_HEADS = 32
SSM_HEAD_DIM = 64
SSM_GROUPS = 8
D_STATE = 128
D_CONV = 4
CONV_DIM = D_INNER + 2 * SSM_GROUPS * D_STATE
SB_HEADS = 16
SB_HEAD_DIM = 64
D_FF = 4 * D_MODEL
DEEPNORM_ALPHA = (2 * DEPTH) ** 0.25
LN_EPS = 1e-5
RMS_EPS = 1e-5

LANES = 128
SUBLANES = 8
HEADS_PER_VREG = LANES // SSM_HEAD_DIM
SSD_CHUNK = 128
ATT_BLOCK = 128
LOG_F32_UNDERFLOW = -104.0
VMEM_LIMIT_CAP = 58 * 1024 * 1024

NT_DIMS = (((1,), (1,)), ((), ()))
TN_DIMS = (((0,), (0,)), ((), ()))


def _softplus(x):
    return jnp.maximum(x, 0.0) + jnp.log1p(jnp.exp(-jnp.abs(x)))


def _silu(x):
    return x * (1.0 / (1.0 + jnp.exp(-x)))


def _layer_norm(x, g, b):
    mu = jnp.mean(x, axis=-1, keepdims=True)
    xc = x - mu
    var = jnp.mean(xc * xc, axis=-1, keepdims=True)
    return xc * lax.rsqrt(var + LN_EPS) * g + b


def _split_bf16(x, parts):
    out = []
    r = x
    for _ in range(parts - 1):
        p = r.astype(BF16)
        out.append(p)
        r = r - p.astype(F32)
    out.append(r.astype(BF16))
    return out


def _dot01_left(m01, x, parts=3):
    acc = None
    for p in _split_bf16(x, parts):
        t = jnp.dot(m01, p, preferred_element_type=F32)
        acc = t if acc is None else acc + t
    return acc


def _dot01_right(x, m01, parts=3):
    acc = None
    for p in _split_bf16(x, parts):
        t = jnp.dot(p, m01, preferred_element_type=F32)
        acc = t if acc is None else acc + t
    return acc


def _resident(shape):
    nd = len(shape)
    return pl.BlockSpec(shape, lambda i: (0,) * nd, pipeline_mode=pl.Buffered(1))


def _rows(tm, width):
    return pl.BlockSpec((tm, width), lambda i: (i, 0))


def _compiler_params(vmem_bytes, semantics=("arbitrary",)):
    return pltpu.CompilerParams(dimension_semantics=semantics,
                                vmem_limit_bytes=min(int(vmem_bytes), VMEM_LIMIT_CAP))


def _ssd_in_kernel(h_ref, wz_ref, wx_ref, wdt_ref, wdtT_ref, cw_ref, cb_ref, dtb_ref, dtbT_ref,
                   z_ref, xs_ref, b_ref, c_ref, dt_ref, dtT_ref, carry_ref, work_ref, *, tm, cchunk):
    hb = h_ref[...].astype(BF16)
    z_ref[...] = jnp.dot(hb, wz_ref[...], preferred_element_type=F32)
    dt_ref[...] = _softplus(jnp.dot(hb, wdt_ref[...], preferred_element_type=F32) + dtb_ref[...])
    dtT_ref[...] = _softplus(
        lax.dot_general(wdtT_ref[...], hb, NT_DIMS, preferred_element_type=F32) + dtbT_ref[...])

    @pl.when(pl.program_id(0) == 0)
    def _():
        carry_ref[...] = jnp.zeros_like(carry_ref)

    for c in range(CONV_DIM // cchunk):
        lo = c * cchunk
        cols = slice(lo, lo + cchunk)
        work_ref[0:SUBLANES, :] = carry_ref[:, cols]
        work_ref[SUBLANES:SUBLANES + tm, :] = jnp.dot(hb, wx_ref[:, cols], preferred_element_type=F32)
        carry_ref[:, cols] = work_ref[tm:tm + SUBLANES, :]
        acc = cb_ref[:, cols]
        for k in range(D_CONV):
            start = SUBLANES - (D_CONV - 1) + k
            acc = acc + cw_ref[k:k + 1, cols] * work_ref[start:start + tm, :]
        y = _silu(acc)
        if lo < D_INNER:
            xs_ref[:, cols] = y
        elif lo < D_INNER + SSM_GROUPS * D_STATE:
            b_ref[:, lo - D_INNER:lo - D_INNER + cchunk] = y.astype(BF16)
        else:
            off = lo - D_INNER - SSM_GROUPS * D_STATE
            c_ref[:, off:off + cchunk] = y.astype(BF16)


def _ssd_in(h, wz, wx, wdt, wdtT, conv_w, conv_b, dtb, dtbT, *, tm=256, cchunk=512):
    L = h.shape[0]
    gn = SSM_GROUPS * D_STATE
    vmem = (2 * (D_MODEL * (D_INNER + CONV_DIM + LANES)) + 2 * tm * D_MODEL * 4
            + 2 * tm * (2 * D_INNER * 4 + 2 * gn * 2 + LANES * 4 + SSM_HEADS * 4)
            + (tm + SUBLANES) * cchunk * 4 + 6 * tm * cchunk * 4 + (8 << 20))
    return pl.pallas_call(
        functools.partial(_ssd_in_kernel, tm=tm, cchunk=cchunk),
        grid=(L // tm,),
        in_specs=[_rows(tm, D_MODEL), _resident(wz.shape), _resident(wx.shape), _resident(wdt.shape),
                  _resident(wdtT.shape), _resident(conv_w.shape), _resident(conv_b.shape),
                  _resident(dtb.shape), _resident(dtbT.shape)],
        out_specs=[_rows(tm, D_INNER), _rows(tm, D_INNER), _rows(tm, gn), _rows(tm, gn),
                   _rows(tm, LANES), pl.BlockSpec((SSM_HEADS, tm), lambda i: (0, i))],
        out_shape=[jax.ShapeDtypeStruct((L, D_INNER), F32), jax.ShapeDtypeStruct((L, D_INNER), F32),
                   jax.ShapeDtypeStruct((L, gn), BF16), jax.ShapeDtypeStruct((L, gn), BF16),
                   jax.ShapeDtypeStruct((L, LANES), F32), jax.ShapeDtypeStruct((SSM_HEADS, L), F32)],
        scratch_shapes=[pltpu.VMEM((SUBLANES, CONV_DIM), F32), pltpu.VMEM((tm + SUBLANES, cchunk), F32)],
        compiler_params=_compiler_params(vmem),
        name="ssd_in",
    )(h, wz, wx, wdt, wdtT, conv_w, conv_b, dtb, dtbT)


def _ssd_scan_kernel(xs_ref, z_ref, b_ref, c_ref, dt_ref, dtT_ref, alog_ref, alogT_ref, dskip_ref,
                     nw_ref, y_ref, state_ref):
    q = SSD_CHUNK

    @pl.when(pl.program_id(0) == 0)
    def _():
        state_ref[...] = jnp.zeros_like(state_ref)

    dt = dt_ref[...]
    dtT = dtT_ref[...]
    da = dt * (-jnp.exp(alog_ref[...]))
    daT = dtT * (-jnp.exp(alogT_ref[...]))
    ri = lax.broadcasted_iota(jnp.int32, (q, q), 0)
    ci = lax.broadcasted_iota(jnp.int32, (q, q), 1)
    tril = ci <= ri
    ltri = jnp.where(tril, 1.0, 0.0).astype(BF16)
    utri = jnp.where(ri <= ci, 1.0, 0.0).astype(BF16)
    cs = _dot01_left(ltri, da)
    csT = _dot01_right(daT, utri)
    cs_last = cs[q - 1:q, :]
    ecs = jnp.exp(cs)
    wdt = jnp.exp(cs_last - cs) * dt
    edec = jnp.exp(cs_last)
    first = lax.broadcasted_iota(jnp.int32, (1, LANES), 1) < SSM_HEAD_DIM

    def per_head(v, h0):
        return jnp.where(first, v[:, h0:h0 + 1], v[:, h0 + 1:h0 + 2])

    for g in range(SSM_GROUPS):
        gcols = slice(g * D_STATE, (g + 1) * D_STATE)
        bg = b_ref[:, gcols]
        cg = c_ref[:, gcols]
        cb = lax.dot_general(cg, bg, NT_DIMS, preferred_element_type=F32)
        gated = []
        for pr in range(2):
            pair = 2 * g + pr
            h0 = HEADS_PER_VREG * pair
            cols = slice(pair * LANES, (pair + 1) * LANES)
            x = xs_ref[:, cols]
            xb = x.astype(BF16)
            halves = []
            for h in (h0, h0 + 1):
                diff = cs[:, h:h + 1] - csT[h:h + 1, :]
                dec = jnp.exp(jnp.where(tril, diff, -jnp.inf))
                m = (cb * dec * dtT[h:h + 1, :]).astype(BF16)
                halves.append(jnp.dot(m, xb, preferred_element_type=F32))
            y = jnp.where(first, halves[0], halves[1])
            st = state_ref[pair]
            y = y + jnp.dot(cg, st.astype(BF16), preferred_element_type=F32) * per_head(ecs, h0)
            xw = (x * per_head(wdt, h0)).astype(BF16)
            upd = lax.dot_general(bg, xw, TN_DIMS, preferred_element_type=F32)
            state_ref[pair] = st * per_head(edec, h0) + upd
            y = y + x * dskip_ref[:, cols]
            gated.append(y * _silu(z_ref[:, cols]))
        ss = (jnp.sum(gated[0] * gated[0], axis=-1, keepdims=True)
              + jnp.sum(gated[1] * gated[1], axis=-1, keepdims=True))
        scale = lax.rsqrt(ss * (1.0 / (2 * LANES)) + RMS_EPS)
        for pr in range(2):
            cols = slice((2 * g + pr) * LANES, (2 * g + pr + 1) * LANES)
            y_ref[:, cols] = (gated[pr] * scale * nw_ref[:, cols]).astype(BF16)


def _ssd_scan(xs, z, bm, cm, dt, dtT, alog, alogT, dskip, nw):
    L = xs.shape[0]
    q = SSD_CHUNK
    gn = SSM_GROUPS * D_STATE
    n_pairs = SSM_HEADS // HEADS_PER_VREG
    return pl.pallas_call(
        _ssd_scan_kernel,
        grid=(L // q,),
        in_specs=[_rows(q, D_INNER), _rows(q, D_INNER), _rows(q, gn), _rows(q, gn), _rows(q, LANES),
                  pl.BlockSpec((SSM_HEADS, q), lambda i: (0, i)), _resident(alog.shape),
                  _resident(alogT.shape), _resident(dskip.shape), _resident(nw.shape)],
        out_specs=_rows(q, D_INNER),
        out_shape=jax.ShapeDtypeStruct((L, D_INNER), BF16),
        scratch_shapes=[pltpu.VMEM((n_pairs, D_STATE, LANES), F32)],
        compiler_params=_compiler_params(32 << 20),
        name="ssd_scan",
    )(xs, z, bm, cm, dt, dtT, alog, alogT, dskip, nw)


def _post_kernel(m_ref, h_ref, wp_ref, g1_ref, b1_ref, w1_ref, w2_ref, g2_ref, b2_ref, out_ref, *, ff_chunk):
    mix = jnp.dot(m_ref[...], wp_ref[...], preferred_element_type=F32)
    h1 = _layer_norm(DEEPNORM_ALPHA * h_ref[...] + mix, g1_ref[...], b1_ref[...])
    hb = h1.astype(BF16)
    acc = None
    for c in range(D_FF // ff_chunk):
        cols = slice(c * ff_chunk, (c + 1) * ff_chunk)
        a = jnp.maximum(jnp.dot(hb, w1_ref[:, cols], preferred_element_type=F32), 0.0)
        t = jnp.dot((a * a).astype(BF16), w2_ref[cols, :], preferred_element_type=F32)
        acc = t if acc is None else acc + t
    out_ref[...] = _layer_norm(DEEPNORM_ALPHA * h1 + acc, g2_ref[...], b2_ref[...])


def _post(m, h, wp, g1, b1, w1, w2, g2, b2, *, tm=256, ff_chunk=1024):
    L, kdim = m.shape
    vmem = (2 * (kdim * D_MODEL + 2 * D_MODEL * D_FF) + 2 * tm * (kdim * 2 + 2 * D_MODEL * 4)
            + 6 * tm * max(ff_chunk, D_MODEL) * 4 + (8 << 20))
    return pl.pallas_call(
        functools.partial(_post_kernel, ff_chunk=ff_chunk),
        grid=(L // tm,),
        in_specs=[_rows(tm, kdim), _rows(tm, D_MODEL), _resident(wp.shape), _resident(g1.shape),
                  _resident(b1.shape), _resident(w1.shape), _resident(w2.shape), _resident(g2.shape),
                  _resident(b2.shape)],
        out_specs=_rows(tm, D_MODEL),
        out_shape=jax.ShapeDtypeStruct((L, D_MODEL), F32),
        compiler_params=_compiler_params(vmem),
        name="post",
    )(m, h, wp, g1, b1, w1, w2, g2, b2)


def _kv_kernel(h_ref, wk_ref, wv_ref, k_ref, v_ref):
    hb = h_ref[...].astype(BF16)
    k_ref[...] = jnp.dot(hb, wk_ref[...], preferred_element_type=F32).astype(BF16)
    v_ref[...] = jnp.dot(hb, wv_ref[...], preferred_element_type=F32).astype(BF16)


def _kv(h, wk, wv, *, tm=512):
    L = h.shape[0]
    vmem = 2 * 2 * D_MODEL * D_MODEL + 2 * tm * D_MODEL * (4 + 2 + 2) + 4 * tm * D_MODEL * 4 + (8 << 20)
    return pl.pallas_call(
        _kv_kernel,
        grid=(L // tm,),
        in_specs=[_rows(tm, D_MODEL), _resident(wk.shape), _resident(wv.shape)],
        out_specs=[_rows(tm, D_MODEL), _rows(tm, D_MODEL)],
        out_shape=[jax.ShapeDtypeStruct((L, D_MODEL), BF16), jax.ShapeDtypeStruct((L, D_MODEL), BF16)],
        compiler_params=_compiler_params(vmem),
        name="kv",
    )(h, wk, wv)


def _attn_kernel(h_ref, wq_ref, tri_ref, k_hbm, v_hbm, o_ref, kbuf, vbuf, sem, q_ref, acc_ref, o_acc_ref,
                 split_ref, logit_ref, w_ref):
    tq = ATT_BLOCK
    n_pairs = SB_HEADS // HEADS_PER_VREG
    i = pl.program_id(0)

    def copies(j, slot):
        rows = pl.ds(pl.multiple_of(j * tq, tq), tq)
        return (pltpu.make_async_copy(k_hbm.at[rows, :], kbuf.at[slot], sem.at[0, slot]),
                pltpu.make_async_copy(v_hbm.at[rows, :], vbuf.at[slot], sem.at[1, slot]))

    def fetch(j, slot):
        for cp in copies(j, slot):
            cp.start()

    def wait(j, slot):
        for cp in copies(j, slot):
            cp.wait()

    fetch(i, 0)

    lane = lax.broadcasted_iota(jnp.int32, (1, LANES), 1)
    first = lane < SB_HEAD_DIM
    q = jnp.dot(h_ref[...].astype(BF16), wq_ref[...], preferred_element_type=F32) * (SB_HEAD_DIM ** -0.5)
    for pair in range(n_pairs):
        qp = q[:, pair * LANES:(pair + 1) * LANES]
        q_ref[2 * pair * tq:(2 * pair + 1) * tq, :] = jnp.where(first, qp, 0.0).astype(BF16)
        q_ref[(2 * pair + 1) * tq:(2 * pair + 2) * tq, :] = jnp.where(first, 0.0, qp).astype(BF16)
    acc_ref[...] = jnp.zeros_like(acc_ref)
    o_acc_ref[...] = jnp.zeros_like(o_acc_ref)

    ri = lax.broadcasted_iota(jnp.int32, (2 * tq, tq), 0) % tq
    ci = lax.broadcasted_iota(jnp.int32, (2 * tq, tq), 1)
    strictly_before = ci < ri

    def pair_rows(pair):
        return slice(2 * pair * tq, (2 * pair + 2) * tq)

    def pair_cols(pair):
        return slice(pair * LANES, (pair + 1) * LANES)

    def process(slot, diag):
        for pair in range(n_pairs):
            rows = pair_rows(pair)
            z = lax.dot_general(q_ref[rows, :], kbuf[slot, :, pair_cols(pair)], NT_DIMS,
                                preferred_element_type=F32)
            sp = jnp.maximum(z, 0.0) + jnp.log(1.0 + jnp.exp(-jnp.abs(z)))
            if diag:
                sp = jnp.where(strictly_before, sp, 0.0)
            hi = sp.astype(BF16)
            split_ref[rows, :LANES] = hi
            split_ref[rows, LANES:] = (sp - hi.astype(F32)).astype(BF16)
            logit_ref[rows, :] = z - sp
        for pair in range(n_pairs):
            rows = pair_rows(pair)
            r = jnp.dot(split_ref[rows, :], tri_ref[...], preferred_element_type=F32)
            acc = acc_ref[rows, :]
            w = jnp.exp(logit_ref[rows, :] - r[:, :LANES] + acc)
            if diag:
                w = jnp.where(strictly_before, w, 0.0)
            w_ref[rows, :] = w.astype(BF16)
            acc_ref[rows, :] = acc - r[:, LANES:]
        for pair in range(n_pairs):
            cols = pair_cols(pair)
            pv = jnp.dot(w_ref[pair_rows(pair), :], vbuf[slot, :, cols], preferred_element_type=F32)
            o_acc_ref[:, cols] += jnp.where(first, pv[:tq], pv[tq:])

    def keep_going():
        return (jnp.max(acc_ref[...]) >= LOG_F32_UNDERFLOW).astype(jnp.int32)

    @pl.when(i > 0)
    def _():
        fetch(i - 1, 1)

    wait(i, 0)
    process(0, True)

    def cond(carry):
        j, _, go = carry
        return jnp.logical_and(j >= 0, go > 0)

    def body(carry):
        j, slot, _ = carry
        wait(j, slot)

        @pl.when(j > 0)
        def _():
            fetch(j - 1, 1 - slot)

        process(slot, False)
        return j - 1, 1 - slot, keep_going()

    j_end, slot_end, _ = lax.while_loop(cond, body, (i - 1, jnp.int32(1), keep_going()))

    @pl.when(j_end >= 0)
    def _():
        wait(j_end, slot_end)

    o_ref[...] = o_acc_ref[...].astype(BF16)


def _attn_tri():
    j = jnp.arange(ATT_BLOCK)
    later = (j[:, None] > j[None, :]).astype(BF16)
    half = jnp.concatenate([later, jnp.ones((ATT_BLOCK, LANES), BF16)], axis=1)
    return jnp.concatenate([half, half], axis=0)


def _attn(h, wq, k, v):
    L = h.shape[0]
    tq = ATT_BLOCK
    n_pairs = SB_HEADS // HEADS_PER_VREG
    tri = _attn_tri()
    return pl.pallas_call(
        _attn_kernel,
        grid=(L // tq,),
        in_specs=[_rows(tq, D_MODEL), _resident(wq.shape), _resident(tri.shape),
                  pl.BlockSpec(memory_space=pl.ANY), pl.BlockSpec(memory_space=pl.ANY)],
        out_specs=_rows(tq, D_MODEL),
        out_shape=jax.ShapeDtypeStruct((L, D_MODEL), BF16),
        scratch_shapes=[pltpu.VMEM((2, tq, D_MODEL), BF16), pltpu.VMEM((2, tq, D_MODEL), BF16),
                        pltpu.SemaphoreType.DMA((2, 2)), pltpu.VMEM((2 * n_pairs * tq, LANES), BF16),
                        pltpu.VMEM((2 * n_pairs * tq, LANES), F32), pltpu.VMEM((tq, D_MODEL), F32),
                        pltpu.VMEM((2 * n_pairs * tq, 2 * LANES), BF16),
                        pltpu.VMEM((2 * n_pairs * tq, LANES), F32),
                        pltpu.VMEM((2 * n_pairs * tq, LANES), BF16)],
        compiler_params=_compiler_params(32 << 20),
        name="attn",
    )(h, wq, tri, k, v)


def _row(v):
    return v.reshape(1, -1).astype(F32)


def _pad_lanes(v, width=LANES):
    return jnp.pad(v, [(0, 0)] * (v.ndim - 1) + [(0, width - v.shape[-1])])


def kernel(x, ssm_w_in, ssm_conv_w, ssm_conv_b, ssm_dt_bias, ssm_a_log, ssm_d, ssm_norm_w, ssm_w_out,
           sb_w_k, sb_w_v, sb_w_q, sb_w_o, mlp_w1, mlp_w2, ln_mix_g, ln_mix_b, ln_mlp_g, ln_mlp_b):
    assert x.shape[0] == 1 and x.shape[2] == D_MODEL
    L = x.shape[1]
    assert L % 512 == 0
    h = x.reshape(L, D_MODEL)
    k = v = None
    for layer in range(DEPTH):
        if layer < N_SSD_LAYERS:
            w_in = ssm_w_in[layer]
            wz = w_in[:, :D_INNER].astype(BF16)
            wx = w_in[:, D_INNER:D_INNER + CONV_DIM].astype(BF16)
            w_dt = w_in[:, D_INNER + CONV_DIM:]
            wdt = _pad_lanes(w_dt).astype(BF16)
            wdtT = w_dt.T.astype(BF16)
            dtb = _pad_lanes(_row(ssm_dt_bias[layer]))
            dtbT = ssm_dt_bias[layer].reshape(-1, 1).astype(F32)
            z, xs, bm, cm, dt, dtT = _ssd_in(h, wz, wx, wdt, wdtT, ssm_conv_w[layer],
                                             _row(ssm_conv_b[layer]), dtb, dtbT)
            alog = _pad_lanes(_row(ssm_a_log[layer]))
            alogT = ssm_a_log[layer].reshape(-1, 1).astype(F32)
            dskip = _row(jnp.repeat(ssm_d[layer], SSM_HEAD_DIM))
            mix_in = _ssd_scan(xs, z, bm, cm, dt, dtT, alog, alogT, dskip, _row(ssm_norm_w[layer]))
            w_proj = ssm_w_out[layer].astype(BF16)
        else:
            if layer == N_SSD_LAYERS:
                k, v = _kv(h, sb_w_k.astype(BF16), sb_w_v.astype(BF16))
            j = layer - N_SSD_LAYERS
            mix_in = _attn(h, sb_w_q[j].astype(BF16), k, v)
            w_proj = sb_w_o[j].astype(BF16)
        h = _post(mix_in, h, w_proj, _row(ln_mix_g[layer]), _row(ln_mix_b[layer]),
                  mlp_w1[layer].astype(BF16), mlp_w2[layer].astype(BF16),
                  _row(ln_mlp_g[layer]), _row(ln_mlp_b[layer]))
    return h.reshape(1, L, D_MODEL)
```

```python
import functools

import jax
import jax.numpy as jnp
from jax import lax
from jax.experimental import pallas as pl
from jax.experimental.pallas import tpu as pltpu

F32 = jnp.float32
BF16 = jnp.bfloat16

D_MODEL = 1024
DEPTH = 4
N_SSD_LAYERS = DEPTH // 2
D_INNER = 2 * D_MODEL
SSM_HEADS = 32
SSM_HEAD_DIM = 64
SSM_GROUPS = 8
D_STATE = 128
D_CONV = 4
CONV_DIM = D_INNER + 2 * SSM_GROUPS * D_STATE
SB_HEADS = 16
SB_HEAD_DIM = 64
D_FF = 4 * D_MODEL
DEEPNORM_ALPHA = (2 * DEPTH) ** 0.25
LN_EPS = 1e-5
RMS_EPS = 1e-5

LANES = 128
SUBLANES = 8
HEADS_PER_VREG = LANES // SSM_HEAD_DIM
CONV_PHASES = 4
SSD_CHUNK = 128
ATT_BLOCK = 128
ATT_RING = 4
LOG_F32_UNDERFLOW = -104.0
VMEM_LIMIT_CAP = 58 * 1024 * 1024

NT_DIMS = (((1,), (1,)), ((), ()))
TN_DIMS = (((0,), (0,)), ((), ()))


def _softplus(x):
    return jnp.maximum(x, 0.0) + jnp.log1p(jnp.exp(-jnp.abs(x)))


def _silu(x):
    half = 0.5 * x
    return half + half * jnp.tanh(half)


def _layer_norm(x, g, b):
    mu = jnp.mean(x, axis=-1, keepdims=True)
    xc = x - mu
    var = jnp.mean(xc * xc, axis=-1, keepdims=True)
    return xc * lax.rsqrt(var + LN_EPS) * g + b


def _split_bf16(x, parts):
    out = []
    r = x
    for _ in range(parts - 1):
        p = r.astype(BF16)
        out.append(p)
        r = r - p.astype(F32)
    out.append(r.astype(BF16))
    return out


def _dot01_left(m01, x, parts=3):
    acc = None
    for p in _split_bf16(x, parts):
        t = jnp.dot(m01, p, preferred_element_type=F32)
        acc = t if acc is None else acc + t
    return acc


def _dot01_right(x, m01, parts=3):
    acc = None
    for p in _split_bf16(x, parts):
        t = jnp.dot(p, m01, preferred_element_type=F32)
        acc = t if acc is None else acc + t
    return acc


def _resident(shape):
    nd = len(shape)
    return pl.BlockSpec(shape, lambda i: (0,) * nd, pipeline_mode=pl.Buffered(1))


def _rows(tm, width):
    return pl.BlockSpec((tm, width), lambda i: (i, 0))


def _compiler_params(vmem_bytes, semantics=("arbitrary",)):
    return pltpu.CompilerParams(dimension_semantics=semantics,
                                vmem_limit_bytes=min(int(vmem_bytes), VMEM_LIMIT_CAP))


def _ssd_in_kernel(h_ref, wz_ref, wx_ref, wdt_ref, wdtT_ref, cw_ref, cb_ref, dtb_ref, dtbT_ref,
                   z_ref, xs_ref, b_ref, c_ref, dt_ref, dtT_ref, carry_ref, work_ref, y_ref, *, tm, cchunk):
    hb = h_ref[...].astype(BF16)
    dt_ref[...] = _softplus(jnp.dot(hb, wdt_ref[...], preferred_element_type=F32) + dtb_ref[...])
    dtT_ref[...] = _softplus(
        lax.dot_general(wdtT_ref[...], hb, NT_DIMS, preferred_element_type=F32) + dtbT_ref[...])

    @pl.when(pl.program_id(0) == 0)
    def _():
        carry_ref[...] = jnp.zeros_like(carry_ref)

    n_chunks = CONV_DIM // cchunk
    z_chunk = D_INNER // n_chunks
    slabs = cchunk // LANES

    def project(c):
        slot = c % 2
        res = jnp.dot(hb, wx_ref[:, c * cchunk:(c + 1) * cchunk], preferred_element_type=F32)
        for s in range(slabs):
            slab = c * slabs + s
            work_ref[slot, s, 0:SUBLANES, :] = carry_ref[slab]
            work_ref[slot, s, SUBLANES:SUBLANES + tm, :] = res[:, s * LANES:(s + 1) * LANES]
            carry_ref[slab] = work_ref[slot, s, tm:tm + SUBLANES, :]

    def conv(c):
        slot = c % 2
        for s in range(slabs):
            lo = c * cchunk + s * LANES
            lanes = slice(lo, lo + LANES)
            for phase in range(CONV_PHASES):
                acc = cb_ref[:, lanes]
                for k in range(D_CONV):
                    start = SUBLANES + phase - (D_CONV - 1) + k
                    u = work_ref[slot, s, pl.ds(start, tm // CONV_PHASES, stride=CONV_PHASES), :]
                    acc = acc + cw_ref[k:k + 1, lanes] * u
                y_ref[s, pl.ds(phase, tm // CONV_PHASES, stride=CONV_PHASES), :] = _silu(acc)
            y = y_ref[s]
            if lo < D_INNER:
                xs_ref[:, lanes] = y
            elif lo < D_INNER + SSM_GROUPS * D_STATE:
                b_ref[:, lo - D_INNER:lo - D_INNER + LANES] = y.astype(BF16)
            else:
                off = lo - D_INNER - SSM_GROUPS * D_STATE
                c_ref[:, off:off + LANES] = y.astype(BF16)

    project(0)
    for c in range(n_chunks):
        if c + 1 < n_chunks:
            project(c + 1)
        zc = slice(c * z_chunk, (c + 1) * z_chunk)
        z_ref[:, zc] = jnp.dot(hb, wz_ref[:, zc], preferred_element_type=F32)
        conv(c)


def _ssd_in(h, wz, wx, wdt, wdtT, conv_w, conv_b, dtb, dtbT, *, tm=512, cchunk=512):
    L = h.shape[0]
    gn = SSM_GROUPS * D_STATE
    vmem = (2 * (D_MODEL * (D_INNER + CONV_DIM + LANES)) + 2 * tm * D_MODEL * 4
            + 2 * tm * (2 * D_INNER * 4 + 2 * gn * 2 + LANES * 4 + SSM_HEADS * 4)
            + 2 * (tm + SUBLANES) * cchunk * 4 + 6 * tm * cchunk * 4 + (8 << 20))
    return pl.pallas_call(
        functools.partial(_ssd_in_kernel, tm=tm, cchunk=cchunk),
        grid=(L // tm,),
        in_specs=[_rows(tm, D_MODEL), _resident(wz.shape), _resident(wx.shape), _resident(wdt.shape),
                  _resident(wdtT.shape), _resident(conv_w.shape), _resident(conv_b.shape),
                  _resident(dtb.shape), _resident(dtbT.shape)],
        out_specs=[_rows(tm, D_INNER), _rows(tm, D_INNER), _rows(tm, gn), _rows(tm, gn),
                   _rows(tm, LANES), pl.BlockSpec((SSM_HEADS, tm), lambda i: (0, i))],
        out_shape=[jax.ShapeDtypeStruct((L, D_INNER), F32), jax.ShapeDtypeStruct((L, D_INNER), F32),
                   jax.ShapeDtypeStruct((L, gn), BF16), jax.ShapeDtypeStruct((L, gn), BF16),
                   jax.ShapeDtypeStruct((L, LANES), F32), jax.ShapeDtypeStruct((SSM_HEADS, L), F32)],
        scratch_shapes=[pltpu.VMEM((CONV_DIM // LANES, SUBLANES, LANES), F32),
                        pltpu.VMEM((2, cchunk // LANES, tm + SUBLANES, LANES), F32),
                        pltpu.VMEM((cchunk // LANES, tm, LANES), F32)],
        compiler_params=_compiler_params(vmem),
        name="ssd_in",
    )(h, wz, wx, wdt, wdtT, conv_w, conv_b, dtb, dtbT)


def _ssd_scan_kernel(xs_ref, z_ref, b_ref, c_ref, dt_ref, dtT_ref, alog_ref, alogT_ref, dskip_ref,
                     nw_ref, sel_ref, y_ref, state_ref, cb_ref, m_ref, ex_ref, g_ref):
    q = SSD_CHUNK
    heads_per_group = SSM_HEADS // SSM_GROUPS
    gw = heads_per_group * SSM_HEAD_DIM

    @pl.when(pl.program_id(0) == 0)
    def _():
        state_ref[...] = jnp.zeros_like(state_ref)

    dt = dt_ref[...]
    dtT = dtT_ref[...]
    da = dt * (-jnp.exp(alog_ref[...]))
    daT = dtT * (-jnp.exp(alogT_ref[...]))
    ri = lax.broadcasted_iota(jnp.int32, (q, q), 0)
    ci = lax.broadcasted_iota(jnp.int32, (q, q), 1)
    tril = ci <= ri
    ltri = jnp.where(tril, 1.0, 0.0).astype(BF16)
    utri = jnp.where(ri <= ci, 1.0, 0.0).astype(BF16)
    cs = _dot01_left(ltri, da)
    csT = _dot01_right(daT, utri)
    cs_last = cs[q - 1:q, :]
    per_head = jnp.concatenate([jnp.exp(cs), jnp.exp(cs_last - cs) * dt], axis=0)
    hi = per_head.astype(BF16)
    per_head_split = jnp.concatenate([hi, (per_head - hi.astype(F32)).astype(BF16)], axis=1)

    for g in range(SSM_GROUPS):
        gcols = slice(g * D_STATE, (g + 1) * D_STATE)
        ex_ref[g] = jnp.dot(per_head_split, sel_ref[g], preferred_element_type=F32)
        cb_ref[g] = lax.dot_general(c_ref[:, gcols], b_ref[:, gcols], NT_DIMS, preferred_element_type=F32)

    for h in range(SSM_HEADS):
        diff = cs[:, h:h + 1] - csT[h:h + 1, :]
        dec = jnp.exp(jnp.where(tril, diff, -jnp.inf))
        m_ref[h] = (cb_ref[h // heads_per_group] * dec * dtT[h:h + 1, :]).astype(BF16)

    first = lax.broadcasted_iota(jnp.int32, (1, LANES), 1) < SSM_HEAD_DIM
    for g in range(SSM_GROUPS):
        cols = slice(g * gw, (g + 1) * gw)
        x = xs_ref[:, cols]
        xb = x.astype(BF16)
        halves = []
        for pr in range(gw // LANES):
            h0 = g * heads_per_group + pr * HEADS_PER_VREG
            xp = xb[:, pr * LANES:(pr + 1) * LANES]
            halves.append(jnp.where(first, jnp.dot(m_ref[h0], xp, preferred_element_type=F32),
                                    jnp.dot(m_ref[h0 + 1], xp, preferred_element_type=F32)))
        y = jnp.concatenate(halves, axis=1)
        st = state_ref[g]
        y = y + jnp.dot(c_ref[:, g * D_STATE:(g + 1) * D_STATE], st.astype(BF16),
                        preferred_element_type=F32) * ex_ref[g, 0:q, :]
        y = y + x * dskip_ref[:, cols]
        g_ref[:, cols] = y * _silu(z_ref[:, cols])

    for g in range(SSM_GROUPS):
        cols = slice(g * gw, (g + 1) * gw)
        xw = (xs_ref[:, cols] * ex_ref[g, q:2 * q, :]).astype(BF16)
        upd = lax.dot_general(b_ref[:, g * D_STATE:(g + 1) * D_STATE], xw, TN_DIMS, preferred_element_type=F32)
        state_ref[g] = state_ref[g] * ex_ref[g, q - 1:q, :] + upd

    for g in range(SSM_GROUPS):
        cols = slice(g * gw, (g + 1) * gw)
        gated = g_ref[:, cols]
        ss = jnp.sum(gated * gated, axis=-1, keepdims=True)
        y_ref[:, cols] = (gated * lax.rsqrt(ss * (1.0 / gw) + RMS_EPS) * nw_ref[:, cols]).astype(BF16)


def _head_spread():
    k = jnp.arange(2 * LANES) % LANES
    j = jnp.arange(SSM_HEADS // SSM_GROUPS * SSM_HEAD_DIM) // SSM_HEAD_DIM
    g = jnp.arange(SSM_GROUPS) * (SSM_HEADS // SSM_GROUPS)
    return (k[None, :, None] == g[:, None, None] + j[None, None, :]).astype(BF16)


def _ssd_scan(xs, z, bm, cm, dt, dtT, alog, alogT, dskip, nw):
    L = xs.shape[0]
    q = SSD_CHUNK
    gn = SSM_GROUPS * D_STATE
    gw = SSM_HEADS // SSM_GROUPS * SSM_HEAD_DIM
    sel = _head_spread()
    return pl.pallas_call(
        _ssd_scan_kernel,
        grid=(L // q,),
        in_specs=[_rows(q, D_INNER), _rows(q, D_INNER), _rows(q, gn), _rows(q, gn), _rows(q, LANES),
                  pl.BlockSpec((SSM_HEADS, q), lambda i: (0, i)), _resident(alog.shape),
                  _resident(alogT.shape), _resident(dskip.shape), _resident(nw.shape), _resident(sel.shape)],
        out_specs=_rows(q, D_INNER),
        out_shape=jax.ShapeDtypeStruct((L, D_INNER), BF16),
        scratch_shapes=[pltpu.VMEM((SSM_GROUPS, D_STATE, gw), F32), pltpu.VMEM((SSM_GROUPS, q, q), F32),
                        pltpu.VMEM((SSM_HEADS, q, q), BF16), pltpu.VMEM((SSM_GROUPS, 2 * q, gw), F32),
                        pltpu.VMEM((q, D_INNER), F32)],
        compiler_params=_compiler_params(32 << 20),
        name="ssd_scan",
    )(xs, z, bm, cm, dt, dtT, alog, alogT, dskip, nw, sel)


def _post_kernel(m_ref, h_ref, wp_ref, g1_ref, b1_ref, w1_ref, w2_ref, g2_ref, b2_ref, out_ref, *, ff_chunk):
    mix = jnp.dot(m_ref[...], wp_ref[...], preferred_element_type=F32)
    h1 = _layer_norm(DEEPNORM_ALPHA * h_ref[...] + mix, g1_ref[...], b1_ref[...])
    hb = h1.astype(BF16)
    acc = None
    for c in range(D_FF // ff_chunk):
        cols = slice(c * ff_chunk, (c + 1) * ff_chunk)
        a = jnp.maximum(jnp.dot(hb, w1_ref[:, cols], preferred_element_type=F32), 0.0)
        t = jnp.dot((a * a).astype(BF16), w2_ref[cols, :], preferred_element_type=F32)
        acc = t if acc is None else acc + t
    out_ref[...] = _layer_norm(DEEPNORM_ALPHA * h1 + acc, g2_ref[...], b2_ref[...])


def _post(m, h, wp, g1, b1, w1, w2, g2, b2, *, tm=512, ff_chunk=1024):
    L, kdim = m.shape
    vmem = (2 * (kdim * D_MODEL + 2 * D_MODEL * D_FF) + 2 * tm * (kdim * 2 + 2 * D_MODEL * 4)
            + 6 * tm * max(ff_chunk, D_MODEL) * 4 + (8 << 20))
    return pl.pallas_call(
        functools.partial(_post_kernel, ff_chunk=ff_chunk),
        grid=(L // tm,),
        in_specs=[_rows(tm, kdim), _rows(tm, D_MODEL), _resident(wp.shape), _resident(g1.shape),
                  _resident(b1.shape), _resident(w1.shape), _resident(w2.shape), _resident(g2.shape),
                  _resident(b2.shape)],
        out_specs=_rows(tm, D_MODEL),
        out_shape=jax.ShapeDtypeStruct((L, D_MODEL), F32),
        compiler_params=_compiler_params(vmem),
        name="post",
    )(m, h, wp, g1, b1, w1, w2, g2, b2)


def _kv_kernel(h_ref, wk_ref, wv_ref, k_ref, v_ref):
    hb = h_ref[...].astype(BF16)
    k_ref[...] = jnp.dot(hb, wk_ref[...], preferred_element_type=F32).astype(BF16)
    v_ref[...] = jnp.dot(hb, wv_ref[...], preferred_element_type=F32).astype(BF16)


def _kv(h, wk, wv, *, tm=512):
    L = h.shape[0]
    vmem = 2 * 2 * D_MODEL * D_MODEL + 2 * tm * D_MODEL * (4 + 2 + 2) + 4 * tm * D_MODEL * 4 + (8 << 20)
    return pl.pallas_call(
        _kv_kernel,
        grid=(L // tm,),
        in_specs=[_rows(tm, D_MODEL), _resident(wk.shape), _resident(wv.shape)],
        out_specs=[_rows(tm, D_MODEL), _rows(tm, D_MODEL)],
        out_shape=[jax.ShapeDtypeStruct((L, D_MODEL), BF16), jax.ShapeDtypeStruct((L, D_MODEL), BF16)],
        compiler_params=_compiler_params(vmem),
        name="kv",
    )(h, wk, wv)


def _attn_kernel(h_ref, wq_ref, tri_ref, k_hbm, v_hbm, o_ref, kbuf, vbuf, sem, q_ref, acc_ref, o_acc_ref,
                 split_ref, logit_ref, w_ref):
    tq = ATT_BLOCK
    n_pairs = SB_HEADS // HEADS_PER_VREG
    i = pl.program_id(0)

    def ring_slot(j):
        return lax.rem(j, ATT_RING)

    def demand_slot(j):
        return ATT_RING + lax.rem(j, 2)

    def copies(j, slot):
        rows = pl.ds(pl.multiple_of(j * tq, tq), tq)
        return (pltpu.make_async_copy(k_hbm.at[rows, :], kbuf.at[slot], sem.at[0, slot]),
                pltpu.make_async_copy(v_hbm.at[rows, :], vbuf.at[slot], sem.at[1, slot]))

    def fetch(j, slot):
        for cp in copies(j, slot):
            cp.start()

    def wait(j, slot):
        for cp in copies(j, slot):
            cp.wait()

    @pl.when(i == 0)
    def _():
        fetch(i, ring_slot(i))

    @pl.when(i + 1 < pl.num_programs(0))
    def _():
        fetch(i + 1, ring_slot(i + 1))

    lane = lax.broadcasted_iota(jnp.int32, (1, LANES), 1)
    first = lane < SB_HEAD_DIM
    q = jnp.dot(h_ref[...].astype(BF16), wq_ref[...], preferred_element_type=F32) * (SB_HEAD_DIM ** -0.5)
    for pair in range(n_pairs):
        qp = q[:, pair * LANES:(pair + 1) * LANES]
        q_ref[2 * pair * tq:(2 * pair + 1) * tq, :] = jnp.where(first, qp, 0.0).astype(BF16)
        q_ref[(2 * pair + 1) * tq:(2 * pair + 2) * tq, :] = jnp.where(first, 0.0, qp).astype(BF16)
    acc_ref[...] = jnp.zeros_like(acc_ref)
    o_acc_ref[...] = jnp.zeros_like(o_acc_ref)

    ri = lax.broadcasted_iota(jnp.int32, (2 * tq, tq), 0) % tq
    ci = lax.broadcasted_iota(jnp.int32, (2 * tq, tq), 1)
    strictly_before = ci < ri

    def pair_rows(pair):
        return slice(2 * pair * tq, (2 * pair + 2) * tq)

    def pair_cols(pair):
        return slice(pair * LANES, (pair + 1) * LANES)

    def process(slot, diag):
        for pair in range(n_pairs):
            rows = pair_rows(pair)
            z = lax.dot_general(q_ref[rows, :], kbuf[slot, :, pair_cols(pair)], NT_DIMS,
                                preferred_element_type=F32)
            sp = jnp.maximum(z, 0.0) + jnp.log(1.0 + jnp.exp(-jnp.abs(z)))
            if diag:
                sp = jnp.where(strictly_before, sp, 0.0)
            hi = sp.astype(BF16)
            split_ref[rows, :LANES] = hi
            split_ref[rows, LANES:] = (sp - hi.astype(F32)).astype(BF16)
            logit_ref[rows, :] = z - sp
        for pair in range(n_pairs):
            rows = pair_rows(pair)
            r = jnp.dot(split_ref[rows, :], tri_ref[...], preferred_element_type=F32)
            acc = acc_ref[rows, :]
            w = jnp.exp(logit_ref[rows, :] - r[:, :LANES] + acc)
            if diag:
                w = jnp.where(strictly_before, w, 0.0)
            w_ref[rows, :] = w.astype(BF16)
            acc_ref[rows, :] = acc - r[:, LANES:]
        go = (jnp.max(acc_ref[...]) >= LOG_F32_UNDERFLOW).astype(jnp.int32)
        for pair in range(n_pairs):
            cols = pair_cols(pair)
            pv = jnp.dot(w_ref[pair_rows(pair), :], vbuf[slot, :, cols], preferred_element_type=F32)
            o_acc_ref[:, cols] += jnp.where(first, pv[:tq], pv[tq:])
        return go

    wait(i, ring_slot(i))
    go_after_diag = process(ring_slot(i), True)

    def on_demand(j):
        return jnp.logical_and(j >= 0, j <= i - (ATT_RING - 1))

    def cond(carry):
        j, go = carry
        return jnp.logical_and(j >= 0, go > 0)

    def body(carry):
        j, _ = carry
        demand = on_demand(j)
        slot = jnp.where(demand, demand_slot(j), ring_slot(j))

        @pl.when(demand)
        def _():
            wait(j, slot)

        @pl.when(on_demand(j - 1))
        def _():
            fetch(j - 1, demand_slot(j - 1))

        return j - 1, process(slot, False)

    j_end, _ = lax.while_loop(cond, body, (i - 1, go_after_diag))

    @pl.when(on_demand(j_end))
    def _():
        wait(j_end, demand_slot(j_end))

    o_ref[...] = o_acc_ref[...].astype(BF16)


def _attn_tri():
    j = jnp.arange(ATT_BLOCK)
    later = (j[:, None] > j[None, :]).astype(BF16)
    half = jnp.concatenate([later, jnp.ones((ATT_BLOCK, LANES), BF16)], axis=1)
    return jnp.concatenate([half, half], axis=0)


def _attn(h, wq, k, v):
    L = h.shape[0]
    tq = ATT_BLOCK
    n_pairs = SB_HEADS // HEADS_PER_VREG
    tri = _attn_tri()
    return pl.pallas_call(
        _attn_kernel,
        grid=(L // tq,),
        in_specs=[_rows(tq, D_MODEL), _resident(wq.shape), _resident(tri.shape),
                  pl.BlockSpec(memory_space=pl.ANY), pl.BlockSpec(memory_space=pl.ANY)],
        out_specs=_rows(tq, D_MODEL),
        out_shape=jax.ShapeDtypeStruct((L, D_MODEL), BF16),
        scratch_shapes=[pltpu.VMEM((ATT_RING + 2, tq, D_MODEL), BF16),
                        pltpu.VMEM((ATT_RING + 2, tq, D_MODEL), BF16),
                        pltpu.SemaphoreType.DMA((2, ATT_RING + 2)), pltpu.VMEM((2 * n_pairs * tq, LANES), BF16),
                        pltpu.VMEM((2 * n_pairs * tq, LANES), F32), pltpu.VMEM((tq, D_MODEL), F32),
                        pltpu.VMEM((2 * n_pairs * tq, 2 * LANES), BF16),
                        pltpu.VMEM((2 * n_pairs * tq, LANES), F32),
                        pltpu.VMEM((2 * n_pairs * tq, LANES), BF16)],
        compiler_params=_compiler_params(32 << 20),
        name="attn",
    )(h, wq, tri, k, v)


def _row(v):
    return v.reshape(1, -1).astype(F32)


def _pad_lanes(v, width=LANES):
    return jnp.pad(v, [(0, 0)] * (v.ndim - 1) + [(0, width - v.shape[-1])])


def kernel(x, ssm_w_in, ssm_conv_w, ssm_conv_b, ssm_dt_bias, ssm_a_log, ssm_d, ssm_norm_w, ssm_w_out,
           sb_w_k, sb_w_v, sb_w_q, sb_w_o, mlp_w1, mlp_w2, ln_mix_g, ln_mix_b, ln_mlp_g, ln_mlp_b):
    assert x.shape[0] == 1 and x.shape[2] == D_MODEL
    L = x.shape[1]
    assert L % 512 == 0
    h = x.reshape(L, D_MODEL)
    k = v = None
    for layer in range(DEPTH):
        if layer < N_SSD_LAYERS:
            w_in = ssm_w_in[layer]
            wz = w_in[:, :D_INNER].astype(BF16)
            wx = w_in[:, D_INNER:D_INNER + CONV_DIM].astype(BF16)
            w_dt = w_in[:, D_INNER + CONV_DIM:]
            wdt = _pad_lanes(w_dt).astype(BF16)
            wdtT = w_dt.T.astype(BF16)
            dtb = _pad_lanes(_row(ssm_dt_bias[layer]))
            dtbT = ssm_dt_bias[layer].reshape(-1, 1).astype(F32)
            z, xs, bm, cm, dt, dtT = _ssd_in(h, wz, wx, wdt, wdtT, ssm_conv_w[layer],
                                             _row(ssm_conv_b[layer]), dtb, dtbT)
            alog = _pad_lanes(_row(ssm_a_log[layer]))
            alogT = ssm_a_log[layer].reshape(-1, 1).astype(F32)
            dskip = _row(jnp.repeat(ssm_d[layer], SSM_HEAD_DIM))
            mix_in = _ssd_scan(xs, z, bm, cm, dt, dtT, alog, alogT, dskip, _row(ssm_norm_w[layer]))
            w_proj = ssm_w_out[layer].astype(BF16)
        else:
            if layer == N_SSD_LAYERS:
                k, v = _kv(h, sb_w_k.astype(BF16), sb_w_v.astype(BF16))
            j = layer - N_SSD_LAYERS
            mix_in = _attn(h, sb_w_q[j].astype(BF16), k, v)
            w_proj = sb_w_o[j].astype(BF16)
        h = _post(mix_in, h, w_proj, _row(ln_mix_g[layer]), _row(ln_mix_b[layer]),
                  mlp_w1[layer].astype(BF16), mlp_w2[layer].astype(BF16),
                  _row(ln_mlp_g[layer]), _row(ln_mlp_b[layer]))
    return h.reshape(1, L, D_MODEL)
```

```python
import functools

import jax
import jax.numpy as jnp
from jax import lax
from jax.experimental import pallas as pl
from jax.experimental.pallas import tpu as pltpu

F32 = jnp.float32
BF16 = jnp.bfloat16

D_MODEL = 1024
DEPTH = 4
N_SSD_LAYERS = DEPTH // 2
D_INNER = 2 * D_MODEL
SSM_HEADS = 32
SSM_HEAD_DIM = 64
SSM_GROUPS = 8
D_STATE = 128
D_CONV = 4
CONV_DIM = D_INNER + 2 * SSM_GROUPS * D_STATE
SB_HEADS = 16
SB_HEAD_DIM = 64
D_FF = 4 * D_MODEL
DEEPNORM_ALPHA = (2 * DEPTH) ** 0.25
LN_EPS = 1e-5
RMS_EPS = 1e-5

LANES = 128
SUBLANES = 8
HEADS_PER_VREG = LANES // SSM_HEAD_DIM
CONV_PHASES = 4
SSD_CHUNK = 128
ATT_BLOCK = 128
ATT_RING = 4
ATT_TOP_ROWS = 32
LOG2_E = 1.4426950408889634
LOG2_F32_UNDERFLOW = -150.0
VMEM_LIMIT_CAP = 58 * 1024 * 1024

NT_DIMS = (((1,), (1,)), ((), ()))
TN_DIMS = (((0,), (0,)), ((), ()))


def _softplus(x):
    return jnp.maximum(x, 0.0) + jnp.log1p(jnp.exp(-jnp.abs(x)))


def _silu(x):
    half = 0.5 * x
    return half + half * jnp.tanh(half)


def _layer_norm(x, g, b):
    mu = jnp.mean(x, axis=-1, keepdims=True)
    xc = x - mu
    var = jnp.mean(xc * xc, axis=-1, keepdims=True)
    return xc * lax.rsqrt(var + LN_EPS) * g + b


def _split_bf16(x, parts):
    out = []
    r = x
    for _ in range(parts - 1):
        p = r.astype(BF16)
        out.append(p)
        r = r - p.astype(F32)
    out.append(r.astype(BF16))
    return out


def _dot01_left(m01, x, parts=3):
    acc = None
    for p in _split_bf16(x, parts):
        t = jnp.dot(m01, p, preferred_element_type=F32)
        acc = t if acc is None else acc + t
    return acc


def _dot01_right(x, m01, parts=3):
    acc = None
    for p in _split_bf16(x, parts):
        t = jnp.dot(p, m01, preferred_element_type=F32)
        acc = t if acc is None else acc + t
    return acc


def _resident(shape):
    nd = len(shape)
    return pl.BlockSpec(shape, lambda i: (0,) * nd, pipeline_mode=pl.Buffered(1))


def _layer_resident(stacked, layer):
    nd = stacked.ndim - 1
    return pl.BlockSpec((None,) + stacked.shape[1:], lambda i: (layer,) + (0,) * nd,
                        pipeline_mode=pl.Buffered(1))


def _rows(tm, width):
    return pl.BlockSpec((tm, width), lambda i: (i, 0))


def _compiler_params(vmem_bytes, semantics=("arbitrary",)):
    return pltpu.CompilerParams(dimension_semantics=semantics,
                                vmem_limit_bytes=min(int(vmem_bytes), VMEM_LIMIT_CAP))


def _ssd_in_kernel(h_ref, win_ref, wdt_ref, wdtT_ref, cw_ref, cb_ref, dtb_ref, dtbT_ref,
                   z_ref, xs_ref, b_ref, c_ref, dt_ref, dtT_ref, carry_ref, work_ref, y_ref, *, tm, cchunk):
    hb = h_ref[...].astype(BF16)
    dt_ref[...] = _softplus(jnp.dot(hb, wdt_ref[...], preferred_element_type=F32) + dtb_ref[...])
    dtT_ref[...] = _softplus(
        lax.dot_general(wdtT_ref[...], hb, NT_DIMS, preferred_element_type=F32) + dtbT_ref[...])

    @pl.when(pl.program_id(0) == 0)
    def _():
        carry_ref[...] = jnp.zeros_like(carry_ref)

    n_chunks = CONV_DIM // cchunk
    z_chunk = D_INNER // n_chunks
    slabs = cchunk // LANES

    def project(c):
        slot = c % 2
        res = jnp.dot(hb, win_ref[:, D_INNER + c * cchunk:D_INNER + (c + 1) * cchunk],
                      preferred_element_type=F32)
        for s in range(slabs):
            slab = c * slabs + s
            work_ref[slot, s, 0:SUBLANES, :] = carry_ref[slab]
            work_ref[slot, s, SUBLANES:SUBLANES + tm, :] = res[:, s * LANES:(s + 1) * LANES]
            carry_ref[slab] = work_ref[slot, s, tm:tm + SUBLANES, :]

    def conv(c):
        slot = c % 2
        for s in range(slabs):
            lo = c * cchunk + s * LANES
            lanes = slice(lo, lo + LANES)
            for phase in range(CONV_PHASES):
                acc = cb_ref[:, lanes]
                for k in range(D_CONV):
                    start = SUBLANES + phase - (D_CONV - 1) + k
                    u = work_ref[slot, s, pl.ds(start, tm // CONV_PHASES, stride=CONV_PHASES), :]
                    acc = acc + cw_ref[k:k + 1, lanes] * u
                y_ref[s, pl.ds(phase, tm // CONV_PHASES, stride=CONV_PHASES), :] = _silu(acc)
            y = y_ref[s]
            if lo < D_INNER:
                xs_ref[:, lanes] = y
            elif lo < D_INNER + SSM_GROUPS * D_STATE:
                b_ref[:, lo - D_INNER:lo - D_INNER + LANES] = y.astype(BF16)
            else:
                off = lo - D_INNER - SSM_GROUPS * D_STATE
                c_ref[:, off:off + LANES] = y.astype(BF16)

    project(0)
    for c in range(n_chunks):
        if c + 1 < n_chunks:
            project(c + 1)
        zc = slice(c * z_chunk, (c + 1) * z_chunk)
        z_ref[:, zc] = _silu(jnp.dot(hb, win_ref[:, zc], preferred_element_type=F32)).astype(BF16)
        conv(c)


def _ssd_in(h, w_in, layer, wdt, wdtT, conv_w, conv_b, dtb, dtbT, *, tm=512, cchunk=512):
    L = h.shape[0]
    gn = SSM_GROUPS * D_STATE
    vmem = (2 * (D_MODEL * (D_INNER + CONV_DIM + LANES)) + 2 * tm * D_MODEL * 4
            + 2 * tm * (2 * D_INNER * 4 + 2 * gn * 2 + LANES * 4 + SSM_HEADS * 4)
            + 2 * (tm + SUBLANES) * cchunk * 4 + 6 * tm * cchunk * 4 + (8 << 20))
    return pl.pallas_call(
        functools.partial(_ssd_in_kernel, tm=tm, cchunk=cchunk),
        grid=(L // tm,),
        in_specs=[_rows(tm, D_MODEL), _layer_resident(w_in, layer), _resident(wdt.shape),
                  _resident(wdtT.shape), _layer_resident(conv_w, layer), _resident(conv_b.shape),
                  _resident(dtb.shape), _resident(dtbT.shape)],
        out_specs=[_rows(tm, D_INNER), _rows(tm, D_INNER), _rows(tm, gn), _rows(tm, gn),
                   _rows(tm, LANES), pl.BlockSpec((SSM_HEADS, tm), lambda i: (0, i))],
        out_shape=[jax.ShapeDtypeStruct((L, D_INNER), BF16), jax.ShapeDtypeStruct((L, D_INNER), F32),
                   jax.ShapeDtypeStruct((L, gn), BF16), jax.ShapeDtypeStruct((L, gn), BF16),
                   jax.ShapeDtypeStruct((L, LANES), F32), jax.ShapeDtypeStruct((SSM_HEADS, L), F32)],
        scratch_shapes=[pltpu.VMEM((CONV_DIM // LANES, SUBLANES, LANES), F32),
                        pltpu.VMEM((2, cchunk // LANES, tm + SUBLANES, LANES), F32),
                        pltpu.VMEM((cchunk // LANES, tm, LANES), F32)],
        compiler_params=_compiler_params(vmem),
        name="ssd_in",
    )(h, w_in, wdt, wdtT, conv_w, conv_b, dtb, dtbT)


def _ssd_scan_kernel(xs_ref, z_ref, b_ref, c_ref, dt_ref, dtT_ref, alog_ref, alogT_ref, dskip_ref,
                     nw_ref, sel_ref, y_ref, state_ref, cb_ref, m_ref, ex_ref, g_ref):
    q = SSD_CHUNK
    heads_per_group = SSM_HEADS // SSM_GROUPS
    gw = heads_per_group * SSM_HEAD_DIM

    @pl.when(pl.program_id(0) == 0)
    def _():
        state_ref[...] = jnp.zeros_like(state_ref)

    dt = dt_ref[...]
    dtT = dtT_ref[...]
    da = dt * (-LOG2_E * jnp.exp(alog_ref[...]))
    daT = dtT * (-LOG2_E * jnp.exp(alogT_ref[...]))
    ri = lax.broadcasted_iota(jnp.int32, (q, q), 0)
    ci = lax.broadcasted_iota(jnp.int32, (q, q), 1)
    tril = ci <= ri
    ltri = jnp.where(tril, 1.0, 0.0).astype(BF16)
    utri = jnp.where(ri <= ci, 1.0, 0.0).astype(BF16)
    cs = _dot01_left(ltri, da)
    csT = _dot01_right(daT, utri) - jnp.log2(dtT)
    cs_last = cs[q - 1:q, :]
    per_head = jnp.concatenate([jnp.exp2(cs), jnp.exp2(cs_last - cs) * dt], axis=0)
    hi = per_head.astype(BF16)
    per_head_split = jnp.concatenate([hi, (per_head - hi.astype(F32)).astype(BF16)], axis=1)

    for g in range(SSM_GROUPS):
        gcols = slice(g * D_STATE, (g + 1) * D_STATE)
        ex_ref[g] = jnp.dot(per_head_split, sel_ref[g], preferred_element_type=F32)
        cb_ref[g] = lax.dot_general(c_ref[:, gcols], b_ref[:, gcols], NT_DIMS, preferred_element_type=F32)

    for h in range(SSM_HEADS):
        diff = cs[:, h:h + 1] - csT[h:h + 1, :]
        dec = jnp.exp2(jnp.where(tril, diff, -jnp.inf))
        m_ref[h] = (cb_ref[h // heads_per_group] * dec).astype(BF16)

    first = lax.broadcasted_iota(jnp.int32, (1, LANES), 1) < SSM_HEAD_DIM
    for g in range(SSM_GROUPS):
        cols = slice(g * gw, (g + 1) * gw)
        x = xs_ref[:, cols]
        xb = x.astype(BF16)
        halves = []
        for pr in range(gw // LANES):
            h0 = g * heads_per_group + pr * HEADS_PER_VREG
            xp = xb[:, pr * LANES:(pr + 1) * LANES]
            halves.append(jnp.where(first, jnp.dot(m_ref[h0], xp, preferred_element_type=F32),
                                    jnp.dot(m_ref[h0 + 1], xp, preferred_element_type=F32)))
        y = jnp.concatenate(halves, axis=1)
        st = state_ref[g]
        y = y + jnp.dot(c_ref[:, g * D_STATE:(g + 1) * D_STATE], st.astype(BF16),
                        preferred_element_type=F32) * ex_ref[g, 0:q, :]
        y = y + x * dskip_ref[:, cols]
        g_ref[:, cols] = y * z_ref[:, cols].astype(F32)

    for g in range(SSM_GROUPS):
        cols = slice(g * gw, (g + 1) * gw)
        xw = (xs_ref[:, cols] * ex_ref[g, q:2 * q, :]).astype(BF16)
        upd = lax.dot_general(b_ref[:, g * D_STATE:(g + 1) * D_STATE], xw, TN_DIMS, preferred_element_type=F32)
        state_ref[g] = state_ref[g] * ex_ref[g, q - 1:q, :] + upd

    for g in range(SSM_GROUPS):
        cols = slice(g * gw, (g + 1) * gw)
        gated = g_ref[:, cols]
        ss = jnp.sum(gated * gated, axis=-1, keepdims=True)
        y_ref[:, cols] = (gated * lax.rsqrt(ss * (1.0 / gw) + RMS_EPS) * nw_ref[:, cols]).astype(BF16)


def _head_spread():
    k = jnp.arange(2 * LANES) % LANES
    j = jnp.arange(SSM_HEADS // SSM_GROUPS * SSM_HEAD_DIM) // SSM_HEAD_DIM
    g = jnp.arange(SSM_GROUPS) * (SSM_HEADS // SSM_GROUPS)
    return (k[None, :, None] == g[:, None, None] + j[None, None, :]).astype(BF16)


def _ssd_scan(xs, z, bm, cm, dt, dtT, alog, alogT, dskip, nw):
    L = xs.shape[0]
    q = SSD_CHUNK
    gn = SSM_GROUPS * D_STATE
    gw = SSM_HEADS // SSM_GROUPS * SSM_HEAD_DIM
    sel = _head_spread()
    return pl.pallas_call(
        _ssd_scan_kernel,
        grid=(L // q,),
        in_specs=[_rows(q, D_INNER), _rows(q, D_INNER), _rows(q, gn), _rows(q, gn), _rows(q, LANES),
                  pl.BlockSpec((SSM_HEADS, q), lambda i: (0, i)), _resident(alog.shape),
                  _resident(alogT.shape), _resident(dskip.shape), _resident(nw.shape), _resident(sel.shape)],
        out_specs=_rows(q, D_INNER),
        out_shape=jax.ShapeDtypeStruct((L, D_INNER), BF16),
        scratch_shapes=[pltpu.VMEM((SSM_GROUPS, D_STATE, gw), F32), pltpu.VMEM((SSM_GROUPS, q, q), F32),
                        pltpu.VMEM((SSM_HEADS, q, q), BF16), pltpu.VMEM((SSM_GROUPS, 2 * q, gw), F32),
                        pltpu.VMEM((q, D_INNER), F32)],
        compiler_params=_compiler_params(32 << 20),
        name="ssd_scan",
    )(xs, z, bm, cm, dt, dtT, alog, alogT, dskip, nw, sel)


def _post_kernel(m_ref, h_ref, wp_ref, g1_ref, b1_ref, w1_ref, w2_ref, g2_ref, b2_ref, out_ref, *, ff_chunk):
    mix = jnp.dot(m_ref[...], wp_ref[...], preferred_element_type=F32)
    h1 = _layer_norm(DEEPNORM_ALPHA * h_ref[...] + mix, g1_ref[...], b1_ref[...])
    hb = h1.astype(BF16)
    acc = None
    for c in range(D_FF // ff_chunk):
        cols = slice(c * ff_chunk, (c + 1) * ff_chunk)
        a = jnp.maximum(jnp.dot(hb, w1_ref[:, cols], preferred_element_type=F32), 0.0)
        t = jnp.dot((a * a).astype(BF16), w2_ref[cols, :], preferred_element_type=F32)
        acc = t if acc is None else acc + t
    out_ref[...] = _layer_norm(DEEPNORM_ALPHA * h1 + acc, g2_ref[...], b2_ref[...])


def _post(m, h, wp, wp_layer, g1, b1, w1, w2, layer, g2, b2, *, tm=512, ff_chunk=1024):
    L, kdim = m.shape
    vmem = (2 * (kdim * D_MODEL + 2 * D_MODEL * D_FF) + 2 * tm * (kdim * 2 + 2 * D_MODEL * 4)
            + 6 * tm * max(ff_chunk, D_MODEL) * 4 + (8 << 20))
    return pl.pallas_call(
        functools.partial(_post_kernel, ff_chunk=ff_chunk),
        grid=(L // tm,),
        in_specs=[_rows(tm, kdim), _rows(tm, D_MODEL), _layer_resident(wp, wp_layer), _resident(g1.shape),
                  _resident(b1.shape), _layer_resident(w1, layer), _layer_resident(w2, layer),
                  _resident(g2.shape), _resident(b2.shape)],
        out_specs=_rows(tm, D_MODEL),
        out_shape=jax.ShapeDtypeStruct((L, D_MODEL), F32),
        compiler_params=_compiler_params(vmem),
        name="post",
    )(m, h, wp, g1, b1, w1, w2, g2, b2)


def _kv_kernel(h_ref, wk_ref, wv_ref, k_ref, v_ref):
    hb = h_ref[...].astype(BF16)
    k_ref[...] = jnp.dot(hb, wk_ref[...], preferred_element_type=F32).astype(BF16)
    v_ref[...] = jnp.dot(hb, wv_ref[...], preferred_element_type=F32).astype(BF16)


def _kv(h, wk, wv, *, tm=512):
    L = h.shape[0]
    vmem = 2 * 2 * D_MODEL * D_MODEL + 2 * tm * D_MODEL * (4 + 2 + 2) + 4 * tm * D_MODEL * 4 + (8 << 20)
    return pl.pallas_call(
        _kv_kernel,
        grid=(L // tm,),
        in_specs=[_rows(tm, D_MODEL), _resident(wk.shape), _resident(wv.shape)],
        out_specs=[_rows(tm, D_MODEL), _rows(tm, D_MODEL)],
        out_shape=[jax.ShapeDtypeStruct((L, D_MODEL), BF16), jax.ShapeDtypeStruct((L, D_MODEL), BF16)],
        compiler_params=_compiler_params(vmem),
        name="kv",
    )(h, wk, wv)


def _attn_kernel(h_ref, wq_ref, tri_ref, k_hbm, v_hbm, o_ref, kbuf, vbuf, sem, q_ref, acc_ref, o_acc_ref,
                 split_ref, logit_ref, w_ref):
    tq = ATT_BLOCK
    n_pairs = SB_HEADS // HEADS_PER_VREG
    i = pl.program_id(0)

    def ring_slot(j):
        return lax.rem(j, ATT_RING)

    def demand_slot(j):
        return ATT_RING + lax.rem(j, 2)

    def copies(j, slot):
        rows = pl.ds(pl.multiple_of(j * tq, tq), tq)
        return (pltpu.make_async_copy(k_hbm.at[rows, :], kbuf.at[slot], sem.at[0, slot]),
                pltpu.make_async_copy(v_hbm.at[rows, :], vbuf.at[slot], sem.at[1, slot]))

    def fetch(j, slot):
        for cp in copies(j, slot):
            cp.start()

    def wait(j, slot):
        for cp in copies(j, slot):
            cp.wait()

    @pl.when(i == 0)
    def _():
        fetch(i, ring_slot(i))

    @pl.when(i + 1 < pl.num_programs(0))
    def _():
        fetch(i + 1, ring_slot(i + 1))

    lane = lax.broadcasted_iota(jnp.int32, (1, LANES), 1)
    first = lane < SB_HEAD_DIM
    q = jnp.dot(h_ref[...].astype(BF16), wq_ref[...], preferred_element_type=F32) * (SB_HEAD_DIM ** -0.5 * LOG2_E)
    for pair in range(n_pairs):
        qp = q[:, pair * LANES:(pair + 1) * LANES]
        q_ref[2 * pair * tq:(2 * pair + 1) * tq, :] = jnp.where(first, qp, 0.0).astype(BF16)
        q_ref[(2 * pair + 1) * tq:(2 * pair + 2) * tq, :] = jnp.where(first, 0.0, qp).astype(BF16)
    acc_ref[...] = jnp.zeros_like(acc_ref)
    o_acc_ref[...] = jnp.zeros_like(o_acc_ref)

    def pair_cols(pair):
        return slice(pair * LANES, (pair + 1) * LANES)

    def process(slot, diag, nrows):
        def load(ref, pair):
            base = 2 * pair * tq
            return jnp.concatenate([ref[base:base + nrows, :], ref[base + tq:base + tq + nrows, :]], axis=0)

        def store(ref, pair, val):
            base = 2 * pair * tq
            ref[base:base + nrows, :] = val[:nrows]
            ref[base + tq:base + tq + nrows, :] = val[nrows:]

        if diag:
            ri = lax.broadcasted_iota(jnp.int32, (2 * nrows, tq), 0) % nrows
            ci = lax.broadcasted_iota(jnp.int32, (2 * nrows, tq), 1)
            strictly_before = ci < ri

        for pair in range(n_pairs):
            z = lax.dot_general(load(q_ref, pair), kbuf[slot, :, pair_cols(pair)], NT_DIMS,
                                preferred_element_type=F32)
            sp = jnp.maximum(z, 0.0) + jnp.log2(1.0 + jnp.exp2(-jnp.abs(z)))
            if diag:
                sp = jnp.where(strictly_before, sp, 0.0)
            hi = sp.astype(BF16)
            store(split_ref, pair, jnp.concatenate([hi, (sp - hi.astype(F32)).astype(BF16)], axis=1))
            store(logit_ref, pair, z - sp)
        live_top = None
        live_rest = None
        for pair in range(n_pairs):
            r = jnp.dot(load(split_ref, pair), tri_ref[...], preferred_element_type=F32)
            acc = load(acc_ref, pair)
            w = jnp.exp2(load(logit_ref, pair) - r[:, :LANES] + acc)
            if diag:
                w = jnp.where(strictly_before, w, 0.0)
            store(w_ref, pair, w.astype(BF16))
            acc = acc - r[:, LANES:]
            store(acc_ref, pair, acc)
            top = jnp.maximum(acc[:ATT_TOP_ROWS], acc[nrows:nrows + ATT_TOP_ROWS])
            live_top = top if live_top is None else jnp.maximum(live_top, top)
            if nrows > ATT_TOP_ROWS:
                rest = jnp.maximum(acc[ATT_TOP_ROWS:nrows], acc[nrows + ATT_TOP_ROWS:])
                live_rest = rest if live_rest is None else jnp.maximum(live_rest, rest)
        go_top = jnp.max(live_top) >= LOG2_F32_UNDERFLOW
        go_rest = (jnp.max(live_rest) >= LOG2_F32_UNDERFLOW) if nrows > ATT_TOP_ROWS else jnp.bool_(False)
        for pair in range(n_pairs):
            cols = pair_cols(pair)
            pv = jnp.dot(load(w_ref, pair), vbuf[slot, :, cols], preferred_element_type=F32)
            o_acc_ref[0:nrows, cols] += jnp.where(first, pv[:nrows], pv[nrows:])
        return jnp.logical_or(go_top, go_rest).astype(jnp.int32), go_rest.astype(jnp.int32)

    wait(i, ring_slot(i))
    flags_after_diag = process(ring_slot(i), True, tq)

    def on_demand(j):
        return jnp.logical_and(j >= 0, j <= i - (ATT_RING - 1))

    def cond(carry):
        j, go, _ = carry
        return jnp.logical_and(j >= 0, go > 0)

    def body(carry):
        j, _, go_rest = carry
        demand = on_demand(j)
        slot = jnp.where(demand, demand_slot(j), ring_slot(j))

        @pl.when(demand)
        def _():
            wait(j, slot)

        @pl.when(on_demand(j - 1))
        def _():
            fetch(j - 1, demand_slot(j - 1))

        go, go_rest = lax.cond(go_rest > 0,
                               lambda: process(slot, False, tq),
                               lambda: process(slot, False, ATT_TOP_ROWS))
        return j - 1, go, go_rest

    j_end, _, _ = lax.while_loop(cond, body, (i - 1,) + flags_after_diag)

    @pl.when(on_demand(j_end))
    def _():
        wait(j_end, demand_slot(j_end))

    o_ref[...] = o_acc_ref[...].astype(BF16)


def _attn_tri():
    j = jnp.arange(ATT_BLOCK)
    later = (j[:, None] > j[None, :]).astype(BF16)
    half = jnp.concatenate([later, jnp.ones((ATT_BLOCK, LANES), BF16)], axis=1)
    return jnp.concatenate([half, half], axis=0)


def _attn(h, wq, wq_layer, k, v):
    L = h.shape[0]
    tq = ATT_BLOCK
    n_pairs = SB_HEADS // HEADS_PER_VREG
    tri = _attn_tri()
    return pl.pallas_call(
        _attn_kernel,
        grid=(L // tq,),
        in_specs=[_rows(tq, D_MODEL), _layer_resident(wq, wq_layer), _resident(tri.shape),
                  pl.BlockSpec(memory_space=pl.ANY), pl.BlockSpec(memory_space=pl.ANY)],
        out_specs=_rows(tq, D_MODEL),
        out_shape=jax.ShapeDtypeStruct((L, D_MODEL), BF16),
        scratch_shapes=[pltpu.VMEM((ATT_RING + 2, tq, D_MODEL), BF16),
                        pltpu.VMEM((ATT_RING + 2, tq, D_MODEL), BF16),
                        pltpu.SemaphoreType.DMA((2, ATT_RING + 2)), pltpu.VMEM((2 * n_pairs * tq, LANES), BF16),
                        pltpu.VMEM((2 * n_pairs * tq, LANES), F32), pltpu.VMEM((tq, D_MODEL), F32),
                        pltpu.VMEM((2 * n_pairs * tq, 2 * LANES), BF16),
                        pltpu.VMEM((2 * n_pairs * tq, LANES), F32),
                        pltpu.VMEM((2 * n_pairs * tq, LANES), BF16)],
        compiler_params=_compiler_params(32 << 20),
        name="attn",
    )(h, wq, tri, k, v)


def _row(v):
    return v.reshape(1, -1).astype(F32)


def _pad_lanes(v, width=LANES):
    return jnp.pad(v, [(0, 0)] * (v.ndim - 1) + [(0, width - v.shape[-1])])


def kernel(x, ssm_w_in, ssm_conv_w, ssm_conv_b, ssm_dt_bias, ssm_a_log, ssm_d, ssm_norm_w, ssm_w_out,
           sb_w_k, sb_w_v, sb_w_q, sb_w_o, mlp_w1, mlp_w2, ln_mix_g, ln_mix_b, ln_mlp_g, ln_mlp_b):
    assert x.shape[0] == 1 and x.shape[2] == D_MODEL
    L = x.shape[1]
    assert L % 512 == 0
    h = x.reshape(L, D_MODEL)
    w_in = ssm_w_in.astype(BF16)
    w_out = ssm_w_out.astype(BF16)
    w_q = sb_w_q.astype(BF16)
    w_o = sb_w_o.astype(BF16)
    w1 = mlp_w1.astype(BF16)
    w2 = mlp_w2.astype(BF16)
    k = v = None
    for layer in range(DEPTH):
        if layer < N_SSD_LAYERS:
            w_dt = ssm_w_in[layer, :, D_INNER + CONV_DIM:]
            wdt = _pad_lanes(w_dt).astype(BF16)
            wdtT = w_dt.T.astype(BF16)
            dtb = _pad_lanes(_row(ssm_dt_bias[layer]))
            dtbT = ssm_dt_bias[layer].reshape(-1, 1).astype(F32)
            z, xs, bm, cm, dt, dtT = _ssd_in(h, w_in, layer, wdt, wdtT, ssm_conv_w,
                                             _row(ssm_conv_b[layer]), dtb, dtbT)
            alog = _pad_lanes(_row(ssm_a_log[layer]))
            alogT = ssm_a_log[layer].reshape(-1, 1).astype(F32)
            dskip = _row(jnp.repeat(ssm_d[layer], SSM_HEAD_DIM))
            mix_in = _ssd_scan(xs, z, bm, cm, dt, dtT, alog, alogT, dskip, _row(ssm_norm_w[layer]))
            w_proj, proj_layer = w_out, layer
        else:
            if layer == N_SSD_LAYERS:
                k, v = _kv(h, sb_w_k.astype(BF16), sb_w_v.astype(BF16))
            j = layer - N_SSD_LAYERS
            mix_in = _attn(h, w_q, j, k, v)
            w_proj, proj_layer = w_o, j
        h = _post(mix_in, h, w_proj, proj_layer, _row(ln_mix_g[layer]), _row(ln_mix_b[layer]),
                  w1, w2, layer, _row(ln_mlp_g[layer]), _row(ln_mlp_b[layer]))
    return h.reshape(1, L, D_MODEL)
```

```python
import functools

import jax
import jax.numpy as jnp
from jax import lax
from jax.experimental import pallas as pl
from jax.experimental.pallas import tpu as pltpu

F32 = jnp.float32
BF16 = jnp.bfloat16

D_MODEL = 1024
DEPTH = 4
N_SSD_LAYERS = DEPTH // 2
D_INNER = 2 * D_MODEL
SSM_HEADS = 32
SSM_HEAD_DIM = 64
SSM_GROUPS = 8
D_STATE = 128
D_CONV = 4
CONV_DIM = D_INNER + 2 * SSM_GROUPS * D_STATE
SB_HEADS = 16
SB_HEAD_DIM = 64
D_FF = 4 * D_MODEL
DEEPNORM_ALPHA = (2 * DEPTH) ** 0.25
LN_EPS = 1e-5
RMS_EPS = 1e-5

LANES = 128
SUBLANES = 8
HEADS_PER_VREG = LANES // SSM_HEAD_DIM
CONV_PHASES = 4
SSD_CHUNK = 128
ATT_BLOCK = 128
ATT_RING = 4
ATT_TOP_ROWS = 32
LOG2_E = 1.4426950408889634
LOG2_F32_UNDERFLOW = -150.0
VMEM_LIMIT_CAP = 58 * 1024 * 1024

NT_DIMS = (((1,), (1,)), ((), ()))
TN_DIMS = (((0,), (0,)), ((), ()))


def _softplus(x):
    return jnp.maximum(x, 0.0) + jnp.log1p(jnp.exp(-jnp.abs(x)))


def _silu(x):
    half = 0.5 * x
    return half + half * jnp.tanh(half)


def _layer_norm(x, g, b):
    mu = jnp.mean(x, axis=-1, keepdims=True)
    xc = x - mu
    var = jnp.mean(xc * xc, axis=-1, keepdims=True)
    return xc * lax.rsqrt(var + LN_EPS) * g + b


def _split_bf16(x, parts):
    out = []
    r = x
    for _ in range(parts - 1):
        p = r.astype(BF16)
        out.append(p)
        r = r - p.astype(F32)
    out.append(r.astype(BF16))
    return out


def _dot01_left(m01, x, parts=3):
    acc = None
    for p in _split_bf16(x, parts):
        t = jnp.dot(m01, p, preferred_element_type=F32)
        acc = t if acc is None else acc + t
    return acc


def _dot01_right(x, m01, parts=3):
    acc = None
    for p in _split_bf16(x, parts):
        t = jnp.dot(p, m01, preferred_element_type=F32)
        acc = t if acc is None else acc + t
    return acc


def _resident(shape):
    nd = len(shape)
    return pl.BlockSpec(shape, lambda i: (0,) * nd, pipeline_mode=pl.Buffered(1))


def _layer_resident(stacked, layer):
    nd = stacked.ndim - 1
    return pl.BlockSpec((None,) + stacked.shape[1:], lambda i: (layer,) + (0,) * nd,
                        pipeline_mode=pl.Buffered(1))


def _rows(tm, width):
    return pl.BlockSpec((tm, width), lambda i: (i, 0))


def _compiler_params(vmem_bytes, semantics=("arbitrary",)):
    return pltpu.CompilerParams(dimension_semantics=semantics,
                                vmem_limit_bytes=min(int(vmem_bytes), VMEM_LIMIT_CAP))


def _ssd_in_kernel(h_ref, win_ref, wdt_ref, wdtT_ref, cw_ref, cb_ref, dtb_ref, dtbT_ref,
                   z_ref, xs_ref, b_ref, c_ref, dt_ref, dtT_ref, carry_ref, work_ref, y_ref, *, tm, cchunk):
    hb = h_ref[...].astype(BF16)
    dt_ref[...] = _softplus(jnp.dot(hb, wdt_ref[...], preferred_element_type=F32) + dtb_ref[...])
    dtT_ref[...] = _softplus(
        lax.dot_general(wdtT_ref[...], hb, NT_DIMS, preferred_element_type=F32) + dtbT_ref[...])

    @pl.when(pl.program_id(0) == 0)
    def _():
        carry_ref[...] = jnp.zeros_like(carry_ref)

    n_chunks = CONV_DIM // cchunk
    z_chunk = D_INNER // n_chunks
    slabs = cchunk // LANES

    def project(c):
        slot = c % 2
        res = jnp.dot(hb, win_ref[:, D_INNER + c * cchunk:D_INNER + (c + 1) * cchunk],
                      preferred_element_type=F32)
        for s in range(slabs):
            slab = c * slabs + s
            work_ref[slot, s, 0:SUBLANES, :] = carry_ref[slab]
            work_ref[slot, s, SUBLANES:SUBLANES + tm, :] = res[:, s * LANES:(s + 1) * LANES]
            carry_ref[slab] = work_ref[slot, s, tm:tm + SUBLANES, :]

    def conv(c):
        slot = c % 2
        for s in range(slabs):
            lo = c * cchunk + s * LANES
            lanes = slice(lo, lo + LANES)
            for phase in range(CONV_PHASES):
                acc = cb_ref[:, lanes]
                for k in range(D_CONV):
                    start = SUBLANES + phase - (D_CONV - 1) + k
                    u = work_ref[slot, s, pl.ds(start, tm // CONV_PHASES, stride=CONV_PHASES), :]
                    acc = acc + cw_ref[k:k + 1, lanes] * u
                y_ref[s, pl.ds(phase, tm // CONV_PHASES, stride=CONV_PHASES), :] = _silu(acc)
            y = y_ref[s]
            if lo < D_INNER:
                xs_ref[:, lanes] = y
            elif lo < D_INNER + SSM_GROUPS * D_STATE:
                b_ref[:, lo - D_INNER:lo - D_INNER + LANES] = y.astype(BF16)
            else:
                off = lo - D_INNER - SSM_GROUPS * D_STATE
                c_ref[:, off:off + LANES] = y.astype(BF16)

    project(0)
    for c in range(n_chunks):
        if c + 1 < n_chunks:
            project(c + 1)
        zc = slice(c * z_chunk, (c + 1) * z_chunk)
        z_ref[:, zc] = _silu(jnp.dot(hb, win_ref[:, zc], preferred_element_type=F32)).astype(BF16)
        conv(c)


def _ssd_in(h, w_in, layer, wdt, wdtT, conv_w, conv_b, dtb, dtbT, *, tm=512, cchunk=512):
    L = h.shape[0]
    gn = SSM_GROUPS * D_STATE
    vmem = (2 * (D_MODEL * (D_INNER + CONV_DIM + LANES)) + 2 * tm * D_MODEL * 4
            + 2 * tm * (2 * D_INNER * 4 + 2 * gn * 2 + LANES * 4 + SSM_HEADS * 4)
            + 2 * (tm + SUBLANES) * cchunk * 4 + 6 * tm * cchunk * 4 + (8 << 20))
    return pl.pallas_call(
        functools.partial(_ssd_in_kernel, tm=tm, cchunk=cchunk),
        grid=(L // tm,),
        in_specs=[_rows(tm, D_MODEL), _layer_resident(w_in, layer), _resident(wdt.shape),
                  _resident(wdtT.shape), _layer_resident(conv_w, layer), _resident(conv_b.shape),
                  _resident(dtb.shape), _resident(dtbT.shape)],
        out_specs=[_rows(tm, D_INNER), _rows(tm, D_INNER), _rows(tm, gn), _rows(tm, gn),
                   _rows(tm, LANES), pl.BlockSpec((SSM_HEADS, tm), lambda i: (0, i))],
        out_shape=[jax.ShapeDtypeStruct((L, D_INNER), BF16), jax.ShapeDtypeStruct((L, D_INNER), F32),
                   jax.ShapeDtypeStruct((L, gn), BF16), jax.ShapeDtypeStruct((L, gn), BF16),
                   jax.ShapeDtypeStruct((L, LANES), F32), jax.ShapeDtypeStruct((SSM_HEADS, L), F32)],
        scratch_shapes=[pltpu.VMEM((CONV_DIM // LANES, SUBLANES, LANES), F32),
                        pltpu.VMEM((2, cchunk // LANES, tm + SUBLANES, LANES), F32),
                        pltpu.VMEM((cchunk // LANES, tm, LANES), F32)],
        compiler_params=_compiler_params(vmem),
        name="ssd_in",
    )(h, w_in, wdt, wdtT, conv_w, conv_b, dtb, dtbT)


def _ssd_scan_kernel(xs_ref, z_ref, b_ref, c_ref, dt_ref, dtT_ref, alog_ref, alogT_ref, dskip_ref,
                     nw_ref, sel_ref, y_ref, state_ref, cb_ref, m_ref, ex_ref, g_ref):
    q = SSD_CHUNK
    heads_per_group = SSM_HEADS // SSM_GROUPS
    gw = heads_per_group * SSM_HEAD_DIM

    @pl.when(pl.program_id(0) == 0)
    def _():
        state_ref[...] = jnp.zeros_like(state_ref)

    dt = dt_ref[...]
    dtT = dtT_ref[...]
    da = dt * (-LOG2_E * jnp.exp(alog_ref[...]))
    daT = dtT * (-LOG2_E * jnp.exp(alogT_ref[...]))
    ri = lax.broadcasted_iota(jnp.int32, (q, q), 0)
    ci = lax.broadcasted_iota(jnp.int32, (q, q), 1)
    tril = ci <= ri
    ltri = jnp.where(tril, 1.0, 0.0).astype(BF16)
    utri = jnp.where(ri <= ci, 1.0, 0.0).astype(BF16)
    cs = _dot01_left(ltri, da)
    csT = _dot01_right(daT, utri) - jnp.log2(dtT)
    cs_last = cs[q - 1:q, :]
    per_head = jnp.concatenate([jnp.exp2(cs), jnp.exp2(cs_last - cs) * dt], axis=0)
    hi = per_head.astype(BF16)
    per_head_split = jnp.concatenate([hi, (per_head - hi.astype(F32)).astype(BF16)], axis=1)

    for g in range(SSM_GROUPS):
        gcols = slice(g * D_STATE, (g + 1) * D_STATE)
        ex_ref[g] = jnp.dot(per_head_split, sel_ref[g], preferred_element_type=F32)
        cb_ref[g] = lax.dot_general(c_ref[:, gcols], b_ref[:, gcols], NT_DIMS, preferred_element_type=F32)

    for h in range(SSM_HEADS):
        diff = cs[:, h:h + 1] - csT[h:h + 1, :]
        dec = jnp.exp2(jnp.where(tril, diff, -jnp.inf))
        m_ref[h] = (cb_ref[h // heads_per_group] * dec).astype(BF16)

    first = lax.broadcasted_iota(jnp.int32, (1, LANES), 1) < SSM_HEAD_DIM
    for g in range(SSM_GROUPS):
        cols = slice(g * gw, (g + 1) * gw)
        x = xs_ref[:, cols]
        xb = x.astype(BF16)
        halves = []
        for pr in range(gw // LANES):
            h0 = g * heads_per_group + pr * HEADS_PER_VREG
            xp = xb[:, pr * LANES:(pr + 1) * LANES]
            halves.append(jnp.where(first, jnp.dot(m_ref[h0], xp, preferred_element_type=F32),
                                    jnp.dot(m_ref[h0 + 1], xp, preferred_element_type=F32)))
        y = jnp.concatenate(halves, axis=1)
        st = state_ref[g]
        y = y + jnp.dot(c_ref[:, g * D_STATE:(g + 1) * D_STATE], st.astype(BF16),
                        preferred_element_type=F32) * ex_ref[g, 0:q, :]
        y = y + x * dskip_ref[:, cols]
        g_ref[:, cols] = y * z_ref[:, cols].astype(F32)

    for g in range(SSM_GROUPS):
        cols = slice(g * gw, (g + 1) * gw)
        xw = (xs_ref[:, cols] * ex_ref[g, q:2 * q, :]).astype(BF16)
        upd = lax.dot_general(b_ref[:, g * D_STATE:(g + 1) * D_STATE], xw, TN_DIMS, preferred_element_type=F32)
        state_ref[g] = state_ref[g] * ex_ref[g, q - 1:q, :] + upd

    for g in range(SSM_GROUPS):
        cols = slice(g * gw, (g + 1) * gw)
        gated = g_ref[:, cols]
        ss = jnp.sum(gated * gated, axis=-1, keepdims=True)
        y_ref[:, cols] = (gated * lax.rsqrt(ss * (1.0 / gw) + RMS_EPS) * nw_ref[:, cols]).astype(BF16)


def _head_spread():
    k = jnp.arange(2 * LANES) % LANES
    j = jnp.arange(SSM_HEADS // SSM_GROUPS * SSM_HEAD_DIM) // SSM_HEAD_DIM
    g = jnp.arange(SSM_GROUPS) * (SSM_HEADS // SSM_GROUPS)
    return (k[None, :, None] == g[:, None, None] + j[None, None, :]).astype(BF16)


def _ssd_scan(xs, z, bm, cm, dt, dtT, alog, alogT, dskip, nw):
    L = xs.shape[0]
    q = SSD_CHUNK
    gn = SSM_GROUPS * D_STATE
    gw = SSM_HEADS // SSM_GROUPS * SSM_HEAD_DIM
    sel = _head_spread()
    return pl.pallas_call(
        _ssd_scan_kernel,
        grid=(L // q,),
        in_specs=[_rows(q, D_INNER), _rows(q, D_INNER), _rows(q, gn), _rows(q, gn), _rows(q, LANES),
                  pl.BlockSpec((SSM_HEADS, q), lambda i: (0, i)), _resident(alog.shape),
                  _resident(alogT.shape), _resident(dskip.shape), _resident(nw.shape), _resident(sel.shape)],
        out_specs=_rows(q, D_INNER),
        out_shape=jax.ShapeDtypeStruct((L, D_INNER), BF16),
        scratch_shapes=[pltpu.VMEM((SSM_GROUPS, D_STATE, gw), F32), pltpu.VMEM((SSM_GROUPS, q, q), F32),
                        pltpu.VMEM((SSM_HEADS, q, q), BF16), pltpu.VMEM((SSM_GROUPS, 2 * q, gw), F32),
                        pltpu.VMEM((q, D_INNER), F32)],
        compiler_params=_compiler_params(32 << 20),
        name="ssd_scan",
    )(xs, z, bm, cm, dt, dtT, alog, alogT, dskip, nw, sel)


def _post_kernel(m_ref, h_ref, wp_ref, g1_ref, b1_ref, w1_ref, w2_ref, g2_ref, b2_ref, out_ref, *, ff_chunk):
    mix = jnp.dot(m_ref[...], wp_ref[...], preferred_element_type=F32)
    h1 = _layer_norm(DEEPNORM_ALPHA * h_ref[...] + mix, g1_ref[...], b1_ref[...])
    hb = h1.astype(BF16)
    acc = None
    for c in range(D_FF // ff_chunk):
        cols = slice(c * ff_chunk, (c + 1) * ff_chunk)
        a = jnp.maximum(jnp.dot(hb, w1_ref[:, cols], preferred_element_type=F32), 0.0)
        t = jnp.dot((a * a).astype(BF16), w2_ref[cols, :], preferred_element_type=F32)
        acc = t if acc is None else acc + t
    out_ref[...] = _layer_norm(DEEPNORM_ALPHA * h1 + acc, g2_ref[...], b2_ref[...])


def _post(m, h, wp, wp_layer, g1, b1, w1, w2, layer, g2, b2, *, tm=512, ff_chunk=1024):
    L, kdim = m.shape
    vmem = (2 * (kdim * D_MODEL + 2 * D_MODEL * D_FF) + 2 * tm * (kdim * 2 + 2 * D_MODEL * 4)
            + 6 * tm * max(ff_chunk, D_MODEL) * 4 + (8 << 20))
    return pl.pallas_call(
        functools.partial(_post_kernel, ff_chunk=ff_chunk),
        grid=(L // tm,),
        in_specs=[_rows(tm, kdim), _rows(tm, D_MODEL), _layer_resident(wp, wp_layer), _resident(g1.shape),
                  _resident(b1.shape), _layer_resident(w1, layer), _layer_resident(w2, layer),
                  _resident(g2.shape), _resident(b2.shape)],
        out_specs=_rows(tm, D_MODEL),
        out_shape=jax.ShapeDtypeStruct((L, D_MODEL), F32),
        compiler_params=_compiler_params(vmem),
        name="post",
    )(m, h, wp, g1, b1, w1, w2, g2, b2)


def _kv_kernel(h_ref, wk_ref, wv_ref, k_ref, v_ref):
    hb = h_ref[...].astype(BF16)
    k_ref[...] = jnp.dot(hb, wk_ref[...], preferred_element_type=F32).astype(BF16)
    v_ref[...] = jnp.dot(hb, wv_ref[...], preferred_element_type=F32).astype(BF16)


def _kv(h, wk, wv, *, tm=512):
    L = h.shape[0]
    vmem = 2 * 2 * D_MODEL * D_MODEL + 2 * tm * D_MODEL * (4 + 2 + 2) + 4 * tm * D_MODEL * 4 + (8 << 20)
    return pl.pallas_call(
        _kv_kernel,
        grid=(L // tm,),
        in_specs=[_rows(tm, D_MODEL), _resident(wk.shape), _resident(wv.shape)],
        out_specs=[_rows(tm, D_MODEL), _rows(tm, D_MODEL)],
        out_shape=[jax.ShapeDtypeStruct((L, D_MODEL), BF16), jax.ShapeDtypeStruct((L, D_MODEL), BF16)],
        compiler_params=_compiler_params(vmem),
        name="kv",
    )(h, wk, wv)


def _attn_kernel(h_ref, wq_ref, tri_ref, k_hbm, v_hbm, o_ref, kbuf, vbuf, sem, q_ref, acc_ref, o_acc_ref,
                 split_ref, logit_ref, w_ref):
    tq = ATT_BLOCK
    n_pairs = SB_HEADS // HEADS_PER_VREG
    i = pl.program_id(0)

    def ring_slot(j):
        return lax.rem(j, ATT_RING)

    def demand_slot(j):
        return ATT_RING + lax.rem(j, 2)

    def copies(j, slot):
        rows = pl.ds(pl.multiple_of(j * tq, tq), tq)
        return (pltpu.make_async_copy(k_hbm.at[rows, :], kbuf.at[slot], sem.at[0, slot]),
                pltpu.make_async_copy(v_hbm.at[rows, :], vbuf.at[slot], sem.at[1, slot]))

    def fetch(j, slot):
        for cp in copies(j, slot):
            cp.start()

    def wait(j, slot):
        for cp in copies(j, slot):
            cp.wait()

    @pl.when(i == 0)
    def _():
        fetch(i, ring_slot(i))

    @pl.when(i + 1 < pl.num_programs(0))
    def _():
        fetch(i + 1, ring_slot(i + 1))

    lane = lax.broadcasted_iota(jnp.int32, (1, LANES), 1)
    first = lane < SB_HEAD_DIM
    q = jnp.dot(h_ref[...].astype(BF16), wq_ref[...], preferred_element_type=F32) * (SB_HEAD_DIM ** -0.5 * LOG2_E)
    for pair in range(n_pairs):
        qp = q[:, pair * LANES:(pair + 1) * LANES]
        q_ref[2 * pair * tq:(2 * pair + 1) * tq, :] = jnp.where(first, qp, 0.0).astype(BF16)
        q_ref[(2 * pair + 1) * tq:(2 * pair + 2) * tq, :] = jnp.where(first, 0.0, qp).astype(BF16)
    acc_ref[...] = jnp.zeros_like(acc_ref)
    o_acc_ref[...] = jnp.zeros_like(o_acc_ref)

    def pair_cols(pair):
        return slice(pair * LANES, (pair + 1) * LANES)

    def process(slot, diag, nrows):
        def load(ref, pair):
            base = 2 * pair * tq
            return jnp.concatenate([ref[base:base + nrows, :], ref[base + tq:base + tq + nrows, :]], axis=0)

        def store(ref, pair, val):
            base = 2 * pair * tq
            ref[base:base + nrows, :] = val[:nrows]
            ref[base + tq:base + tq + nrows, :] = val[nrows:]

        if diag:
            ri = lax.broadcasted_iota(jnp.int32, (2 * nrows, tq), 0) % nrows
            ci = lax.broadcasted_iota(jnp.int32, (2 * nrows, tq), 1)
            strictly_before = ci < ri

        for pair in range(n_pairs):
            z = lax.dot_general(load(q_ref, pair), kbuf[slot, :, pair_cols(pair)], NT_DIMS,
                                preferred_element_type=F32)
            sp = jnp.maximum(z, 0.0) + jnp.log2(1.0 + jnp.exp2(-jnp.abs(z)))
            if diag:
                sp = jnp.where(strictly_before, sp, 0.0)
            hi = sp.astype(BF16)
            store(split_ref, pair, jnp.concatenate([hi, (sp - hi.astype(F32)).astype(BF16)], axis=1))
            store(logit_ref, pair, z - sp)
        live_top = None
        live_rest = None
        for pair in range(n_pairs):
            r = jnp.dot(load(split_ref, pair), tri_ref[...], preferred_element_type=F32)
            acc = load(acc_ref, pair)
            w = jnp.exp2(load(logit_ref, pair) - r[:, :LANES] + acc)
            if diag:
                w = jnp.where(strictly_before, w, 0.0)
            store(w_ref, pair, w.astype(BF16))
            acc = acc - r[:, LANES:]
            store(acc_ref, pair, acc)
            top = jnp.maximum(acc[:ATT_TOP_ROWS], acc[nrows:nrows + ATT_TOP_ROWS])
            live_top = top if live_top is None else jnp.maximum(live_top, top)
            if nrows > ATT_TOP_ROWS:
                rest = jnp.maximum(acc[ATT_TOP_ROWS:nrows], acc[nrows + ATT_TOP_ROWS:])
                live_rest = rest if live_rest is None else jnp.maximum(live_rest, rest)
        go_top = jnp.max(live_top) >= LOG2_F32_UNDERFLOW
        go_rest = (jnp.max(live_rest) >= LOG2_F32_UNDERFLOW) if nrows > ATT_TOP_ROWS else jnp.bool_(False)
        for pair in range(n_pairs):
            cols = pair_cols(pair)
            pv = jnp.dot(load(w_ref, pair), vbuf[slot, :, cols], preferred_element_type=F32)
            o_acc_ref[0:nrows, cols] += jnp.where(first, pv[:nrows], pv[nrows:])
        return jnp.logical_or(go_top, go_rest).astype(jnp.int32), go_rest.astype(jnp.int32)

    wait(i, ring_slot(i))
    flags_after_diag = process(ring_slot(i), True, tq)

    def on_demand(j):
        return jnp.logical_and(j >= 0, j <= i - (ATT_RING - 1))

    def in_flight(j):
        return jnp.logical_and(on_demand(j), on_demand(j + 1))

    def cond(carry):
        j, go, _ = carry
        return jnp.logical_and(j >= 0, go > 0)

    def body(carry):
        j, _, go_rest = carry
        demand = on_demand(j)
        slot = jnp.where(demand, demand_slot(j), ring_slot(j))

        @pl.when(jnp.logical_and(demand, jnp.logical_not(in_flight(j))))
        def _():
            fetch(j, slot)

        @pl.when(demand)
        def _():
            wait(j, slot)

        @pl.when(in_flight(j - 1))
        def _():
            fetch(j - 1, demand_slot(j - 1))

        go, go_rest = lax.cond(go_rest > 0,
                               lambda: process(slot, False, tq),
                               lambda: process(slot, False, ATT_TOP_ROWS))
        return j - 1, go, go_rest

    j_end, _, _ = lax.while_loop(cond, body, (i - 1,) + flags_after_diag)

    @pl.when(in_flight(j_end))
    def _():
        wait(j_end, demand_slot(j_end))

    o_ref[...] = o_acc_ref[...].astype(BF16)


def _attn_tri():
    j = jnp.arange(ATT_BLOCK)
    later = (j[:, None] > j[None, :]).astype(BF16)
    half = jnp.concatenate([later, jnp.ones((ATT_BLOCK, LANES), BF16)], axis=1)
    return jnp.concatenate([half, half], axis=0)


def _attn(h, wq, wq_layer, k, v):
    L = h.shape[0]
    tq = ATT_BLOCK
    n_pairs = SB_HEADS // HEADS_PER_VREG
    tri = _attn_tri()
    return pl.pallas_call(
        _attn_kernel,
        grid=(L // tq,),
        in_specs=[_rows(tq, D_MODEL), _layer_resident(wq, wq_layer), _resident(tri.shape),
                  pl.BlockSpec(memory_space=pl.ANY), pl.BlockSpec(memory_space=pl.ANY)],
        out_specs=_rows(tq, D_MODEL),
        out_shape=jax.ShapeDtypeStruct((L, D_MODEL), BF16),
        scratch_shapes=[pltpu.VMEM((ATT_RING + 2, tq, D_MODEL), BF16),
                        pltpu.VMEM((ATT_RING + 2, tq, D_MODEL), BF16),
                        pltpu.SemaphoreType.DMA((2, ATT_RING + 2)), pltpu.VMEM((2 * n_pairs * tq, LANES), BF16),
                        pltpu.VMEM((2 * n_pairs * tq, LANES), F32), pltpu.VMEM((tq, D_MODEL), F32),
                        pltpu.VMEM((2 * n_pairs * tq, 2 * LANES), BF16),
                        pltpu.VMEM((2 * n_pairs * tq, LANES), F32),
                        pltpu.VMEM((2 * n_pairs * tq, LANES), BF16)],
        compiler_params=_compiler_params(32 << 20),
        name="attn",
    )(h, wq, tri, k, v)


def _row(v):
    return v.reshape(1, -1).astype(F32)


def _pad_lanes(v, width=LANES):
    return jnp.pad(v, [(0, 0)] * (v.ndim - 1) + [(0, width - v.shape[-1])])


def kernel(x, ssm_w_in, ssm_conv_w, ssm_conv_b, ssm_dt_bias, ssm_a_log, ssm_d, ssm_norm_w, ssm_w_out,
           sb_w_k, sb_w_v, sb_w_q, sb_w_o, mlp_w1, mlp_w2, ln_mix_g, ln_mix_b, ln_mlp_g, ln_mlp_b):
    assert x.shape[0] == 1 and x.shape[2] == D_MODEL
    L = x.shape[1]
    assert L % 512 == 0
    h = x.reshape(L, D_MODEL)
    w_in = ssm_w_in.astype(BF16)
    w_out = ssm_w_out.astype(BF16)
    w_q = sb_w_q.astype(BF16)
    w_o = sb_w_o.astype(BF16)
    w1 = mlp_w1.astype(BF16)
    w2 = mlp_w2.astype(BF16)
    k = v = None
    for layer in range(DEPTH):
        if layer < N_SSD_LAYERS:
            w_dt = ssm_w_in[layer, :, D_INNER + CONV_DIM:]
            wdt = _pad_lanes(w_dt).astype(BF16)
            wdtT = w_dt.T.astype(BF16)
            dtb = _pad_lanes(_row(ssm_dt_bias[layer]))
            dtbT = ssm_dt_bias[layer].reshape(-1, 1).astype(F32)
            z, xs, bm, cm, dt, dtT = _ssd_in(h, w_in, layer, wdt, wdtT, ssm_conv_w,
                                             _row(ssm_conv_b[layer]), dtb, dtbT)
            alog = _pad_lanes(_row(ssm_a_log[layer]))
            alogT = ssm_a_log[layer].reshape(-1, 1).astype(F32)
            dskip = _row(jnp.repeat(ssm_d[layer], SSM_HEAD_DIM))
            mix_in = _ssd_scan(xs, z, bm, cm, dt, dtT, alog, alogT, dskip, _row(ssm_norm_w[layer]))
            w_proj, proj_layer = w_out, layer
        else:
            if layer == N_SSD_LAYERS:
                k, v = _kv(h, sb_w_k.astype(BF16), sb_w_v.astype(BF16))
            j = layer - N_SSD_LAYERS
            mix_in = _attn(h, w_q, j, k, v)
            w_proj, proj_layer = w_o, j
        h = _post(mix_in, h, w_proj, proj_layer, _row(ln_mix_g[layer]), _row(ln_mix_b[layer]),
                  w1, w2, layer, _row(ln_mlp_g[layer]), _row(ln_mlp_b[layer]))
    return h.reshape(1, L, D_MODEL)
```

```python
import functools

import jax
import jax.numpy as jnp
from jax import lax
from jax.experimental import pallas as pl
from jax.experimental.pallas import tpu as pltpu

F32 = jnp.float32
BF16 = jnp.bfloat16

D_MODEL = 1024
DEPTH = 4
N_SSD_LAYERS = DEPTH // 2
D_INNER = 2 * D_MODEL
SSM_HEADS = 32
SSM_HEAD_DIM = 64
SSM_GROUPS = 8
D_STATE = 128
D_CONV = 4
CONV_DIM = D_INNER + 2 * SSM_GROUPS * D_STATE
SB_HEADS = 16
SB_HEAD_DIM = 64
D_FF = 4 * D_MODEL
DEEPNORM_ALPHA = (2 * DEPTH) ** 0.25
LN_EPS = 1e-5
RMS_EPS = 1e-5

LANES = 128
SUBLANES = 8
HEADS_PER_VREG = LANES // SSM_HEAD_DIM
CONV_PHASES = 4
SSD_CHUNK = 128
ATT_BLOCK = 128
ATT_RING = 4
ATT_TOP_ROWS = 64
LOG2_E = 1.4426950408889634
LOG2_F32_UNDERFLOW = -150.0
VMEM_LIMIT_CAP = 58 * 1024 * 1024

NT_DIMS = (((1,), (1,)), ((), ()))
TN_DIMS = (((0,), (0,)), ((), ()))


def _softplus(x):
    return jnp.maximum(x, 0.0) + jnp.log1p(jnp.exp(-jnp.abs(x)))


def _silu(x):
    half = 0.5 * x
    return half + half * jnp.tanh(half)


def _layer_norm(x, g, b):
    mu = jnp.mean(x, axis=-1, keepdims=True)
    xc = x - mu
    var = jnp.mean(xc * xc, axis=-1, keepdims=True)
    return xc * lax.rsqrt(var + LN_EPS) * g + b


def _split_bf16(x, parts):
    out = []
    r = x
    for _ in range(parts - 1):
        p = r.astype(BF16)
        out.append(p)
        r = r - p.astype(F32)
    out.append(r.astype(BF16))
    return out


def _dot01_left(m01, x, parts=3):
    acc = None
    for p in _split_bf16(x, parts):
        t = jnp.dot(m01, p, preferred_element_type=F32)
        acc = t if acc is None else acc + t
    return acc


def _dot01_right(x, m01, parts=3):
    acc = None
    for p in _split_bf16(x, parts):
        t = jnp.dot(p, m01, preferred_element_type=F32)
        acc = t if acc is None else acc + t
    return acc


def _resident(shape):
    nd = len(shape)
    return pl.BlockSpec(shape, lambda i: (0,) * nd, pipeline_mode=pl.Buffered(1))


def _layer_resident(stacked, layer):
    nd = stacked.ndim - 1
    return pl.BlockSpec((None,) + stacked.shape[1:], lambda i: (layer,) + (0,) * nd,
                        pipeline_mode=pl.Buffered(1))


def _rows(tm, width):
    return pl.BlockSpec((tm, width), lambda i: (i, 0))


def _compiler_params(vmem_bytes, semantics=("arbitrary",)):
    return pltpu.CompilerParams(dimension_semantics=semantics,
                                vmem_limit_bytes=min(int(vmem_bytes), VMEM_LIMIT_CAP))


def _ssd_in_kernel(h_ref, win_ref, wdt_ref, wdtT_ref, cw_ref, cb_ref, dtb_ref, dtbT_ref,
                   z_ref, xs_ref, b_ref, c_ref, dt_ref, dtT_ref, carry_ref, work_ref, y_ref, *, tm, cchunk):
    hb = h_ref[...].astype(BF16)
    dt_ref[...] = _softplus(jnp.dot(hb, wdt_ref[...], preferred_element_type=F32) + dtb_ref[...])
    dtT_ref[...] = _softplus(
        lax.dot_general(wdtT_ref[...], hb, NT_DIMS, preferred_element_type=F32) + dtbT_ref[...])

    @pl.when(pl.program_id(0) == 0)
    def _():
        carry_ref[...] = jnp.zeros_like(carry_ref)

    n_chunks = CONV_DIM // cchunk
    z_chunk = D_INNER // n_chunks
    slabs = cchunk // LANES

    def project(c):
        slot = c % 2
        res = jnp.dot(hb, win_ref[:, D_INNER + c * cchunk:D_INNER + (c + 1) * cchunk],
                      preferred_element_type=F32)
        for s in range(slabs):
            slab = c * slabs + s
            work_ref[slot, s, 0:SUBLANES, :] = carry_ref[slab]
            work_ref[slot, s, SUBLANES:SUBLANES + tm, :] = res[:, s * LANES:(s + 1) * LANES]
            carry_ref[slab] = work_ref[slot, s, tm:tm + SUBLANES, :]

    def conv(c):
        slot = c % 2
        for s in range(slabs):
            lo = c * cchunk + s * LANES
            lanes = slice(lo, lo + LANES)
            for phase in range(CONV_PHASES):
                acc = cb_ref[:, lanes]
                for k in range(D_CONV):
                    start = SUBLANES + phase - (D_CONV - 1) + k
                    u = work_ref[slot, s, pl.ds(start, tm // CONV_PHASES, stride=CONV_PHASES), :]
                    acc = acc + cw_ref[k:k + 1, lanes] * u
                y_ref[s, pl.ds(phase, tm // CONV_PHASES, stride=CONV_PHASES), :] = _silu(acc)
            y = y_ref[s]
            if lo < D_INNER:
                xs_ref[:, lanes] = y
            elif lo < D_INNER + SSM_GROUPS * D_STATE:
                b_ref[:, lo - D_INNER:lo - D_INNER + LANES] = y.astype(BF16)
            else:
                off = lo - D_INNER - SSM_GROUPS * D_STATE
                c_ref[:, off:off + LANES] = y.astype(BF16)

    project(0)
    for c in range(n_chunks):
        if c + 1 < n_chunks:
            project(c + 1)
        zc = slice(c * z_chunk, (c + 1) * z_chunk)
        z_ref[:, zc] = _silu(jnp.dot(hb, win_ref[:, zc], preferred_element_type=F32)).astype(BF16)
        conv(c)


def _ssd_in(h, w_in, layer, wdt, wdtT, conv_w, conv_b, dtb, dtbT, *, tm=512, cchunk=512):
    L = h.shape[0]
    gn = SSM_GROUPS * D_STATE
    vmem = (2 * (D_MODEL * (D_INNER + CONV_DIM + LANES)) + 2 * tm * D_MODEL * 4
            + 2 * tm * (2 * D_INNER * 4 + 2 * gn * 2 + LANES * 4 + SSM_HEADS * 4)
            + 2 * (tm + SUBLANES) * cchunk * 4 + 6 * tm * cchunk * 4 + (8 << 20))
    return pl.pallas_call(
        functools.partial(_ssd_in_kernel, tm=tm, cchunk=cchunk),
        grid=(L // tm,),
        in_specs=[_rows(tm, D_MODEL), _layer_resident(w_in, layer), _resident(wdt.shape),
                  _resident(wdtT.shape), _layer_resident(conv_w, layer), _resident(conv_b.shape),
                  _resident(dtb.shape), _resident(dtbT.shape)],
        out_specs=[_rows(tm, D_INNER), _rows(tm, D_INNER), _rows(tm, gn), _rows(tm, gn),
                   _rows(tm, LANES), pl.BlockSpec((SSM_HEADS, tm), lambda i: (0, i))],
        out_shape=[jax.ShapeDtypeStruct((L, D_INNER), BF16), jax.ShapeDtypeStruct((L, D_INNER), F32),
                   jax.ShapeDtypeStruct((L, gn), BF16), jax.ShapeDtypeStruct((L, gn), BF16),
                   jax.ShapeDtypeStruct((L, LANES), F32), jax.ShapeDtypeStruct((SSM_HEADS, L), F32)],
        scratch_shapes=[pltpu.VMEM((CONV_DIM // LANES, SUBLANES, LANES), F32),
                        pltpu.VMEM((2, cchunk // LANES, tm + SUBLANES, LANES), F32),
                        pltpu.VMEM((cchunk // LANES, tm, LANES), F32)],
        compiler_params=_compiler_params(vmem),
        name="ssd_in",
    )(h, w_in, wdt, wdtT, conv_w, conv_b, dtb, dtbT)


def _ssd_scan_kernel(xs_ref, z_ref, b_ref, c_ref, dt_ref, dtT_ref, alog_ref, alogT_ref, dskip_ref,
                     nw_ref, sel_ref, y_ref, state_ref, cb_ref, m_ref, ex_ref, g_ref):
    q = SSD_CHUNK
    heads_per_group = SSM_HEADS // SSM_GROUPS
    gw = heads_per_group * SSM_HEAD_DIM

    @pl.when(pl.program_id(0) == 0)
    def _():
        state_ref[...] = jnp.zeros_like(state_ref)

    dt = dt_ref[...]
    dtT = dtT_ref[...]
    da = dt * (-LOG2_E * jnp.exp(alog_ref[...]))
    daT = dtT * (-LOG2_E * jnp.exp(alogT_ref[...]))
    ri = lax.broadcasted_iota(jnp.int32, (q, q), 0)
    ci = lax.broadcasted_iota(jnp.int32, (q, q), 1)
    tril = ci <= ri
    ltri = jnp.where(tril, 1.0, 0.0).astype(BF16)
    utri = jnp.where(ri <= ci, 1.0, 0.0).astype(BF16)
    cs = _dot01_left(ltri, da)
    csT = _dot01_right(daT, utri) - jnp.log2(dtT)
    cs_last = cs[q - 1:q, :]
    per_head = jnp.concatenate([jnp.exp2(cs), jnp.exp2(cs_last - cs) * dt], axis=0)
    hi = per_head.astype(BF16)
    per_head_split = jnp.concatenate([hi, (per_head - hi.astype(F32)).astype(BF16)], axis=1)

    for g in range(SSM_GROUPS):
        gcols = slice(g * D_STATE, (g + 1) * D_STATE)
        ex_ref[g] = jnp.dot(per_head_split, sel_ref[g], preferred_element_type=F32)
        cb_ref[g] = lax.dot_general(c_ref[:, gcols], b_ref[:, gcols], NT_DIMS, preferred_element_type=F32)

    for h in range(SSM_HEADS):
        diff = cs[:, h:h + 1] - csT[h:h + 1, :]
        dec = jnp.exp2(jnp.where(tril, diff, -jnp.inf))
        m_ref[h] = (cb_ref[h // heads_per_group] * dec).astype(BF16)

    first = lax.broadcasted_iota(jnp.int32, (1, LANES), 1) < SSM_HEAD_DIM
    for g in range(SSM_GROUPS):
        cols = slice(g * gw, (g + 1) * gw)
        x = xs_ref[:, cols]
        xb = x.astype(BF16)
        halves = []
        for pr in range(gw // LANES):
            h0 = g * heads_per_group + pr * HEADS_PER_VREG
            xp = xb[:, pr * LANES:(pr + 1) * LANES]
            halves.append(jnp.where(first, jnp.dot(m_ref[h0], xp, preferred_element_type=F32),
                                    jnp.dot(m_ref[h0 + 1], xp, preferred_element_type=F32)))
        y = jnp.concatenate(halves, axis=1)
        st = state_ref[g]
        y = y + jnp.dot(c_ref[:, g * D_STATE:(g + 1) * D_STATE], st.astype(BF16),
                        preferred_element_type=F32) * ex_ref[g, 0:q, :]
        y = y + x * dskip_ref[:, cols]
        g_ref[:, cols] = y * z_ref[:, cols].astype(F32)

    for g in range(SSM_GROUPS):
        cols = slice(g * gw, (g + 1) * gw)
        xw = (xs_ref[:, cols] * ex_ref[g, q:2 * q, :]).astype(BF16)
        upd = lax.dot_general(b_ref[:, g * D_STATE:(g + 1) * D_STATE], xw, TN_DIMS, preferred_element_type=F32)
        state_ref[g] = state_ref[g] * ex_ref[g, q - 1:q, :] + upd

    for g in range(SSM_GROUPS):
        cols = slice(g * gw, (g + 1) * gw)
        gated = g_ref[:, cols]
        ss = jnp.sum(gated * gated, axis=-1, keepdims=True)
        y_ref[:, cols] = (gated * lax.rsqrt(ss * (1.0 / gw) + RMS_EPS) * nw_ref[:, cols]).astype(BF16)


def _head_spread():
    k = jnp.arange(2 * LANES) % LANES
    j = jnp.arange(SSM_HEADS // SSM_GROUPS * SSM_HEAD_DIM) // SSM_HEAD_DIM
    g = jnp.arange(SSM_GROUPS) * (SSM_HEADS // SSM_GROUPS)
    return (k[None, :, None] == g[:, None, None] + j[None, None, :]).astype(BF16)


def _ssd_scan(xs, z, bm, cm, dt, dtT, alog, alogT, dskip, nw):
    L = xs.shape[0]
    q = SSD_CHUNK
    gn = SSM_GROUPS * D_STATE
    gw = SSM_HEADS // SSM_GROUPS * SSM_HEAD_DIM
    sel = _head_spread()
    return pl.pallas_call(
        _ssd_scan_kernel,
        grid=(L // q,),
        in_specs=[_rows(q, D_INNER), _rows(q, D_INNER), _rows(q, gn), _rows(q, gn), _rows(q, LANES),
                  pl.BlockSpec((SSM_HEADS, q), lambda i: (0, i)), _resident(alog.shape),
                  _resident(alogT.shape), _resident(dskip.shape), _resident(nw.shape), _resident(sel.shape)],
        out_specs=_rows(q, D_INNER),
        out_shape=jax.ShapeDtypeStruct((L, D_INNER), BF16),
        scratch_shapes=[pltpu.VMEM((SSM_GROUPS, D_STATE, gw), F32), pltpu.VMEM((SSM_GROUPS, q, q), F32),
                        pltpu.VMEM((SSM_HEADS, q, q), BF16), pltpu.VMEM((SSM_GROUPS, 2 * q, gw), F32),
                        pltpu.VMEM((q, D_INNER), F32)],
        compiler_params=_compiler_params(32 << 20),
        name="ssd_scan",
    )(xs, z, bm, cm, dt, dtT, alog, alogT, dskip, nw, sel)


def _post_kernel(m_ref, h_ref, wp_ref, g1_ref, b1_ref, w1_ref, w2_ref, g2_ref, b2_ref, out_ref, *, ff_chunk):
    mix = jnp.dot(m_ref[...], wp_ref[...], preferred_element_type=F32)
    h1 = _layer_norm(DEEPNORM_ALPHA * h_ref[...] + mix, g1_ref[...], b1_ref[...])
    hb = h1.astype(BF16)
    acc = None
    for c in range(D_FF // ff_chunk):
        cols = slice(c * ff_chunk, (c + 1) * ff_chunk)
        a = jnp.maximum(jnp.dot(hb, w1_ref[:, cols], preferred_element_type=F32), 0.0)
        t = jnp.dot((a * a).astype(BF16), w2_ref[cols, :], preferred_element_type=F32)
        acc = t if acc is None else acc + t
    out_ref[...] = _layer_norm(DEEPNORM_ALPHA * h1 + acc, g2_ref[...], b2_ref[...])


def _post(m, h, wp, wp_layer, g1, b1, w1, w2, layer, g2, b2, *, tm=512, ff_chunk=1024):
    L, kdim = m.shape
    vmem = (2 * (kdim * D_MODEL + 2 * D_MODEL * D_FF) + 2 * tm * (kdim * 2 + 2 * D_MODEL * 4)
            + 6 * tm * max(ff_chunk, D_MODEL) * 4 + (8 << 20))
    return pl.pallas_call(
        functools.partial(_post_kernel, ff_chunk=ff_chunk),
        grid=(L // tm,),
        in_specs=[_rows(tm, kdim), _rows(tm, D_MODEL), _layer_resident(wp, wp_layer), _resident(g1.shape),
                  _resident(b1.shape), _layer_resident(w1, layer), _layer_resident(w2, layer),
                  _resident(g2.shape), _resident(b2.shape)],
        out_specs=_rows(tm, D_MODEL),
        out_shape=jax.ShapeDtypeStruct((L, D_MODEL), F32),
        compiler_params=_compiler_params(vmem),
        name="post",
    )(m, h, wp, g1, b1, w1, w2, g2, b2)


def _kv_kernel(h_ref, wk_ref, wv_ref, k_ref, v_ref):
    hb = h_ref[...].astype(BF16)
    k_ref[...] = jnp.dot(hb, wk_ref[...], preferred_element_type=F32).astype(BF16)
    v_ref[...] = jnp.dot(hb, wv_ref[...], preferred_element_type=F32).astype(BF16)


def _kv(h, wk, wv, *, tm=512):
    L = h.shape[0]
    vmem = 2 * 2 * D_MODEL * D_MODEL + 2 * tm * D_MODEL * (4 + 2 + 2) + 4 * tm * D_MODEL * 4 + (8 << 20)
    return pl.pallas_call(
        _kv_kernel,
        grid=(L // tm,),
        in_specs=[_rows(tm, D_MODEL), _resident(wk.shape), _resident(wv.shape)],
        out_specs=[_rows(tm, D_MODEL), _rows(tm, D_MODEL)],
        out_shape=[jax.ShapeDtypeStruct((L, D_MODEL), BF16), jax.ShapeDtypeStruct((L, D_MODEL), BF16)],
        compiler_params=_compiler_params(vmem),
        name="kv",
    )(h, wk, wv)


def _attn_kernel(h_ref, wq_ref, tri_ref, k_hbm, v_hbm, o_ref, kbuf, vbuf, sem, q_ref, acc_ref, o_acc_ref,
                 split_ref, logit_ref, w_ref):
    tq = ATT_BLOCK
    n_pairs = SB_HEADS // HEADS_PER_VREG
    i = pl.program_id(0)

    def ring_slot(j):
        return lax.rem(j, ATT_RING)

    def demand_slot(j):
        return ATT_RING + lax.rem(j, 2)

    def copies(j, slot):
        rows = pl.ds(pl.multiple_of(j * tq, tq), tq)
        return (pltpu.make_async_copy(k_hbm.at[rows, :], kbuf.at[slot], sem.at[0, slot]),
                pltpu.make_async_copy(v_hbm.at[rows, :], vbuf.at[slot], sem.at[1, slot]))

    def fetch(j, slot):
        for cp in copies(j, slot):
            cp.start()

    def wait(j, slot):
        for cp in copies(j, slot):
            cp.wait()

    @pl.when(i == 0)
    def _():
        fetch(i, ring_slot(i))

    @pl.when(i + 1 < pl.num_programs(0))
    def _():
        fetch(i + 1, ring_slot(i + 1))

    lane = lax.broadcasted_iota(jnp.int32, (1, LANES), 1)
    first = lane < SB_HEAD_DIM
    q = jnp.dot(h_ref[...].astype(BF16), wq_ref[...], preferred_element_type=F32) * (SB_HEAD_DIM ** -0.5 * LOG2_E)
    for pair in range(n_pairs):
        qp = q[:, pair * LANES:(pair + 1) * LANES]
        q_ref[2 * pair * tq:(2 * pair + 1) * tq, :] = jnp.where(first, qp, 0.0).astype(BF16)
        q_ref[(2 * pair + 1) * tq:(2 * pair + 2) * tq, :] = jnp.where(first, 0.0, qp).astype(BF16)
    acc_ref[...] = jnp.zeros_like(acc_ref)
    o_acc_ref[...] = jnp.zeros_like(o_acc_ref)

    def pair_cols(pair):
        return slice(pair * LANES, (pair + 1) * LANES)

    def process(slot, diag, nrows):
        def load(ref, pair):
            base = 2 * pair * tq
            return jnp.concatenate([ref[base:base + nrows, :], ref[base + tq:base + tq + nrows, :]], axis=0)

        def store(ref, pair, val):
            base = 2 * pair * tq
            ref[base:base + nrows, :] = val[:nrows]
            ref[base + tq:base + tq + nrows, :] = val[nrows:]

        if diag:
            ri = lax.broadcasted_iota(jnp.int32, (2 * nrows, tq), 0) % nrows
            ci = lax.broadcasted_iota(jnp.int32, (2 * nrows, tq), 1)
            strictly_before = ci < ri

        for pair in range(n_pairs):
            z = lax.dot_general(load(q_ref, pair), kbuf[slot, :, pair_cols(pair)], NT_DIMS,
                                preferred_element_type=F32)
            sp = jnp.maximum(z, 0.0) + jnp.log2(1.0 + jnp.exp2(-jnp.abs(z)))
            if diag:
                sp = jnp.where(strictly_before, sp, 0.0)
            hi = sp.astype(BF16)
            store(split_ref, pair, jnp.concatenate([hi, (sp - hi.astype(F32)).astype(BF16)], axis=1))
            store(logit_ref, pair, z - sp)
        live_top = None
        live_rest = None
        for pair in range(n_pairs):
            r = jnp.dot(load(split_ref, pair), tri_ref[...], preferred_element_type=F32)
            acc = load(acc_ref, pair)
            w = jnp.exp2(load(logit_ref, pair) - r[:, :LANES] + acc)
            if diag:
                w = jnp.where(strictly_before, w, 0.0)
            store(w_ref, pair, w.astype(BF16))
            acc = acc - r[:, LANES:]
            store(acc_ref, pair, acc)
            top = jnp.maximum(acc[:ATT_TOP_ROWS], acc[nrows:nrows + ATT_TOP_ROWS])
            live_top = top if live_top is None else jnp.maximum(live_top, top)
            if nrows > ATT_TOP_ROWS:
                rest = jnp.maximum(acc[ATT_TOP_ROWS:nrows], acc[nrows + ATT_TOP_ROWS:])
                live_rest = rest if live_rest is None else jnp.maximum(live_rest, rest)
        go_top = jnp.max(live_top) >= LOG2_F32_UNDERFLOW
        go_rest = (jnp.max(live_rest) >= LOG2_F32_UNDERFLOW) if nrows > ATT_TOP_ROWS else jnp.bool_(False)
        for pair in range(n_pairs):
            cols = pair_cols(pair)
            pv = jnp.dot(load(w_ref, pair), vbuf[slot, :, cols], preferred_element_type=F32)
            o_acc_ref[0:nrows, cols] += jnp.where(first, pv[:nrows], pv[nrows:])
        return jnp.logical_or(go_top, go_rest).astype(jnp.int32), go_rest.astype(jnp.int32)

    wait(i, ring_slot(i))
    flags_after_diag = process(ring_slot(i), True, tq)

    def on_demand(j):
        return jnp.logical_and(j >= 0, j <= i - (ATT_RING - 1))

    def in_flight(j):
        return jnp.logical_and(on_demand(j), on_demand(j + 1))

    def cond(carry):
        j, go, _ = carry
        return jnp.logical_and(j >= 0, go > 0)

    def body(carry):
        j, _, go_rest = carry
        demand = on_demand(j)
        slot = jnp.where(demand, demand_slot(j), ring_slot(j))

        @pl.when(jnp.logical_and(demand, jnp.logical_not(in_flight(j))))
        def _():
            fetch(j, slot)

        @pl.when(demand)
        def _():
            wait(j, slot)

        @pl.when(in_flight(j - 1))
        def _():
            fetch(j - 1, demand_slot(j - 1))

        go, go_rest = lax.cond(go_rest > 0,
                               lambda: process(slot, False, tq),
                               lambda: process(slot, False, ATT_TOP_ROWS))
        return j - 1, go, go_rest

    j_end, _, _ = lax.while_loop(cond, body, (i - 1,) + flags_after_diag)

    @pl.when(in_flight(j_end))
    def _():
        wait(j_end, demand_slot(j_end))

    o_ref[...] = o_acc_ref[...].astype(BF16)


def _attn_tri():
    j = jnp.arange(ATT_BLOCK)
    later = (j[:, None] > j[None, :]).astype(BF16)
    half = jnp.concatenate([later, jnp.ones((ATT_BLOCK, LANES), BF16)], axis=1)
    return jnp.concatenate([half, half], axis=0)


def _attn(h, wq, wq_layer, k, v):
    L = h.shape[0]
    tq = ATT_BLOCK
    n_pairs = SB_HEADS // HEADS_PER_VREG
    tri = _attn_tri()
    return pl.pallas_call(
        _attn_kernel,
        grid=(L // tq,),
        in_specs=[_rows(tq, D_MODEL), _layer_resident(wq, wq_layer), _resident(tri.shape),
                  pl.BlockSpec(memory_space=pl.ANY), pl.BlockSpec(memory_space=pl.ANY)],
        out_specs=_rows(tq, D_MODEL),
        out_shape=jax.ShapeDtypeStruct((L, D_MODEL), BF16),
        scratch_shapes=[pltpu.VMEM((ATT_RING + 2, tq, D_MODEL), BF16),
                        pltpu.VMEM((ATT_RING + 2, tq, D_MODEL), BF16),
                        pltpu.SemaphoreType.DMA((2, ATT_RING + 2)), pltpu.VMEM((2 * n_pairs * tq, LANES), BF16),
                        pltpu.VMEM((2 * n_pairs * tq, LANES), F32), pltpu.VMEM((tq, D_MODEL), F32),
                        pltpu.VMEM((2 * n_pairs * tq, 2 * LANES), BF16),
                        pltpu.VMEM((2 * n_pairs * tq, LANES), F32),
                        pltpu.VMEM((2 * n_pairs * tq, LANES), BF16)],
        compiler_params=_compiler_params(32 << 20),
        name="attn",
    )(h, wq, tri, k, v)


def _row(v):
    return v.reshape(1, -1).astype(F32)


def _pad_lanes(v, width=LANES):
    return jnp.pad(v, [(0, 0)] * (v.ndim - 1) + [(0, width - v.shape[-1])])


def kernel(x, ssm_w_in, ssm_conv_w, ssm_conv_b, ssm_dt_bias, ssm_a_log, ssm_d, ssm_norm_w, ssm_w_out,
           sb_w_k, sb_w_v, sb_w_q, sb_w_o, mlp_w1, mlp_w2, ln_mix_g, ln_mix_b, ln_mlp_g, ln_mlp_b):
    assert x.shape[0] == 1 and x.shape[2] == D_MODEL
    L = x.shape[1]
    assert L % 512 == 0
    h = x.reshape(L, D_MODEL)
    w_in = ssm_w_in.astype(BF16)
    w_out = ssm_w_out.astype(BF16)
    w_q = sb_w_q.astype(BF16)
    w_o = sb_w_o.astype(BF16)
    w1 = mlp_w1.astype(BF16)
    w2 = mlp_w2.astype(BF16)
    k = v = None
    for layer in range(DEPTH):
        if layer < N_SSD_LAYERS:
            w_dt = ssm_w_in[layer, :, D_INNER + CONV_DIM:]
            wdt = _pad_lanes(w_dt).astype(BF16)
            wdtT = w_dt.T.astype(BF16)
            dtb = _pad_lanes(_row(ssm_dt_bias[layer]))
            dtbT = ssm_dt_bias[layer].reshape(-1, 1).astype(F32)
            z, xs, bm, cm, dt, dtT = _ssd_in(h, w_in, layer, wdt, wdtT, ssm_conv_w,
                                             _row(ssm_conv_b[layer]), dtb, dtbT)
            alog = _pad_lanes(_row(ssm_a_log[layer]))
            alogT = ssm_a_log[layer].reshape(-1, 1).astype(F32)
            dskip = _row(jnp.repeat(ssm_d[layer], SSM_HEAD_DIM))
            mix_in = _ssd_scan(xs, z, bm, cm, dt, dtT, alog, alogT, dskip, _row(ssm_norm_w[layer]))
            w_proj, proj_layer = w_out, layer
        else:
            if layer == N_SSD_LAYERS:
                k, v = _kv(h, sb_w_k.astype(BF16), sb_w_v.astype(BF16))
            j = layer - N_SSD_LAYERS
            mix_in = _attn(h, w_q, j, k, v)
            w_proj, proj_layer = w_o, j
        h = _post(mix_in, h, w_proj, proj_layer, _row(ln_mix_g[layer]), _row(ln_mix_b[layer]),
                  w1, w2, layer, _row(ln_mlp_g[layer]), _row(ln_mlp_b[layer]))
    return h.reshape(1, L, D_MODEL)
```

```python
import functools

import jax
import jax.numpy as jnp
from jax import lax
from jax.experimental import pallas as pl
from jax.experimental.pallas import tpu as pltpu

F32 = jnp.float32
BF16 = jnp.bfloat16

D_MODEL = 1024
DEPTH = 4
N_SSD_LAYERS = DEPTH // 2
D_INNER = 2 * D_MODEL
SSM_HEADS = 32
SSM_HEAD_DIM = 64
SSM_GROUPS = 8
D_STATE = 128
D_CONV = 4
CONV_DIM = D_INNER + 2 * SSM_GROUPS * D_STATE
SB_HEADS = 16
SB_HEAD_DIM = 64
D_FF = 4 * D_MODEL
DEEPNORM_ALPHA = (2 * DEPTH) ** 0.25
LN_EPS = 1e-5
RMS_EPS = 1e-5

LANES = 128
SUBLANES = 8
HEADS_PER_VREG = LANES // SSM_HEAD_DIM
CONV_PHASES = 4
SSD_CHUNK = 128
ATT_BLOCK = 128
ATT_BLOCKS_PER_STEP = 2
ATT_RING = 4
ATT_TOP_ROWS = 64
LOG2_E = 1.4426950408889634
LOG2_F32_UNDERFLOW = -150.0
VMEM_LIMIT_CAP = 58 * 1024 * 1024

NT_DIMS = (((1,), (1,)), ((), ()))
TN_DIMS = (((0,), (0,)), ((), ()))


def _softplus(x):
    return jnp.maximum(x, 0.0) + jnp.log1p(jnp.exp(-jnp.abs(x)))


def _silu(x):
    half = 0.5 * x
    return half + half * jnp.tanh(half)


def _layer_norm(x, g, b):
    mu = jnp.mean(x, axis=-1, keepdims=True)
    xc = x - mu
    var = jnp.mean(xc * xc, axis=-1, keepdims=True)
    return xc * lax.rsqrt(var + LN_EPS) * g + b


def _split_bf16(x, parts):
    out = []
    r = x
    for _ in range(parts - 1):
        p = r.astype(BF16)
        out.append(p)
        r = r - p.astype(F32)
    out.append(r.astype(BF16))
    return out


def _dot01_left(m01, x, parts=3):
    acc = None
    for p in _split_bf16(x, parts):
        t = jnp.dot(m01, p, preferred_element_type=F32)
        acc = t if acc is None else acc + t
    return acc


def _dot01_right(x, m01, parts=3):
    acc = None
    for p in _split_bf16(x, parts):
        t = jnp.dot(p, m01, preferred_element_type=F32)
        acc = t if acc is None else acc + t
    return acc


def _resident(shape):
    nd = len(shape)
    return pl.BlockSpec(shape, lambda i: (0,) * nd, pipeline_mode=pl.Buffered(1))


def _layer_resident(stacked, layer):
    nd = stacked.ndim - 1
    return pl.BlockSpec((None,) + stacked.shape[1:], lambda i: (layer,) + (0,) * nd,
                        pipeline_mode=pl.Buffered(1))


def _rows(tm, width):
    return pl.BlockSpec((tm, width), lambda i: (i, 0))


def _compiler_params(vmem_bytes, semantics=("arbitrary",)):
    return pltpu.CompilerParams(dimension_semantics=semantics,
                                vmem_limit_bytes=min(int(vmem_bytes), VMEM_LIMIT_CAP))


def _ssd_in_kernel(h_ref, win_ref, wdt_ref, wdtT_ref, cw_ref, cb_ref, dtb_ref, dtbT_ref,
                   z_ref, xs_ref, b_ref, c_ref, dt_ref, dtT_ref, carry_ref, work_ref, y_ref, *, tm, cchunk):
    hb = h_ref[...].astype(BF16)
    dt_ref[...] = _softplus(jnp.dot(hb, wdt_ref[...], preferred_element_type=F32) + dtb_ref[...])
    dtT_ref[...] = _softplus(
        lax.dot_general(wdtT_ref[...], hb, NT_DIMS, preferred_element_type=F32) + dtbT_ref[...])

    @pl.when(pl.program_id(0) == 0)
    def _():
        carry_ref[...] = jnp.zeros_like(carry_ref)

    n_chunks = CONV_DIM // cchunk
    z_chunk = D_INNER // n_chunks
    slabs = cchunk // LANES

    def project(c):
        slot = c % 2
        res = jnp.dot(hb, win_ref[:, D_INNER + c * cchunk:D_INNER + (c + 1) * cchunk],
                      preferred_element_type=F32)
        for s in range(slabs):
            slab = c * slabs + s
            work_ref[slot, s, 0:SUBLANES, :] = carry_ref[slab]
            work_ref[slot, s, SUBLANES:SUBLANES + tm, :] = res[:, s * LANES:(s + 1) * LANES]
            carry_ref[slab] = work_ref[slot, s, tm:tm + SUBLANES, :]

    def conv(c):
        slot = c % 2
        for s in range(slabs):
            lo = c * cchunk + s * LANES
            lanes = slice(lo, lo + LANES)
            for phase in range(CONV_PHASES):
                acc = cb_ref[:, lanes]
                for k in range(D_CONV):
                    start = SUBLANES + phase - (D_CONV - 1) + k
                    u = work_ref[slot, s, pl.ds(start, tm // CONV_PHASES, stride=CONV_PHASES), :]
                    acc = acc + cw_ref[k:k + 1, lanes] * u
                y_ref[s, pl.ds(phase, tm // CONV_PHASES, stride=CONV_PHASES), :] = _silu(acc)
            y = y_ref[s]
            if lo < D_INNER:
                xs_ref[:, lanes] = y
            elif lo < D_INNER + SSM_GROUPS * D_STATE:
                b_ref[:, lo - D_INNER:lo - D_INNER + LANES] = y.astype(BF16)
            else:
                off = lo - D_INNER - SSM_GROUPS * D_STATE
                c_ref[:, off:off + LANES] = y.astype(BF16)

    project(0)
    for c in range(n_chunks):
        if c + 1 < n_chunks:
            project(c + 1)
        zc = slice(c * z_chunk, (c + 1) * z_chunk)
        z_ref[:, zc] = _silu(jnp.dot(hb, win_ref[:, zc], preferred_element_type=F32)).astype(BF16)
        conv(c)


def _ssd_in(h, w_in, layer, wdt, wdtT, conv_w, conv_b, dtb, dtbT, *, tm=512, cchunk=512):
    L = h.shape[0]
    gn = SSM_GROUPS * D_STATE
    vmem = (2 * (D_MODEL * (D_INNER + CONV_DIM + LANES)) + 2 * tm * D_MODEL * 4
            + 2 * tm * (2 * D_INNER * 4 + 2 * gn * 2 + LANES * 4 + SSM_HEADS * 4)
            + 2 * (tm + SUBLANES) * cchunk * 4 + 6 * tm * cchunk * 4 + (8 << 20))
    return pl.pallas_call(
        functools.partial(_ssd_in_kernel, tm=tm, cchunk=cchunk),
        grid=(L // tm,),
        in_specs=[_rows(tm, D_MODEL), _layer_resident(w_in, layer), _resident(wdt.shape),
                  _resident(wdtT.shape), _layer_resident(conv_w, layer), _resident(conv_b.shape),
                  _resident(dtb.shape), _resident(dtbT.shape)],
        out_specs=[_rows(tm, D_INNER), _rows(tm, D_INNER), _rows(tm, gn), _rows(tm, gn),
                   _rows(tm, LANES), pl.BlockSpec((SSM_HEADS, tm), lambda i: (0, i))],
        out_shape=[jax.ShapeDtypeStruct((L, D_INNER), BF16), jax.ShapeDtypeStruct((L, D_INNER), F32),
                   jax.ShapeDtypeStruct((L, gn), BF16), jax.ShapeDtypeStruct((L, gn), BF16),
                   jax.ShapeDtypeStruct((L, LANES), F32), jax.ShapeDtypeStruct((SSM_HEADS, L), F32)],
        scratch_shapes=[pltpu.VMEM((CONV_DIM // LANES, SUBLANES, LANES), F32),
                        pltpu.VMEM((2, cchunk // LANES, tm + SUBLANES, LANES), F32),
                        pltpu.VMEM((cchunk // LANES, tm, LANES), F32)],
        compiler_params=_compiler_params(vmem),
        name="ssd_in",
    )(h, w_in, wdt, wdtT, conv_w, conv_b, dtb, dtbT)


def _ssd_scan_kernel(xs_ref, z_ref, b_ref, c_ref, dt_ref, dtT_ref, alog_ref, alogT_ref, dskip_ref,
                     nw_ref, sel_ref, y_ref, state_ref, cb_ref, m_ref, ex_ref, g_ref):
    q = SSD_CHUNK
    heads_per_group = SSM_HEADS // SSM_GROUPS
    gw = heads_per_group * SSM_HEAD_DIM

    @pl.when(pl.program_id(0) == 0)
    def _():
        state_ref[...] = jnp.zeros_like(state_ref)

    dt = dt_ref[...]
    dtT = dtT_ref[...]
    da = dt * (-LOG2_E * jnp.exp(alog_ref[...]))
    daT = dtT * (-LOG2_E * jnp.exp(alogT_ref[...]))
    ri = lax.broadcasted_iota(jnp.int32, (q, q), 0)
    ci = lax.broadcasted_iota(jnp.int32, (q, q), 1)
    tril = ci <= ri
    ltri = jnp.where(tril, 1.0, 0.0).astype(BF16)
    utri = jnp.where(ri <= ci, 1.0, 0.0).astype(BF16)
    cs = _dot01_left(ltri, da)
    csT = _dot01_right(daT, utri) - jnp.log2(dtT)
    cs_last = cs[q - 1:q, :]
    per_head = jnp.concatenate([jnp.exp2(cs), jnp.exp2(cs_last - cs) * dt], axis=0)
    hi = per_head.astype(BF16)
    per_head_split = jnp.concatenate([hi, (per_head - hi.astype(F32)).astype(BF16)], axis=1)

    for g in range(SSM_GROUPS):
        gcols = slice(g * D_STATE, (g + 1) * D_STATE)
        ex_ref[g] = jnp.dot(per_head_split, sel_ref[g], preferred_element_type=F32)
        cb_ref[g] = lax.dot_general(c_ref[:, gcols], b_ref[:, gcols], NT_DIMS, preferred_element_type=F32)

    for h in range(SSM_HEADS):
        diff = cs[:, h:h + 1] - csT[h:h + 1, :]
        dec = jnp.exp2(jnp.where(tril, diff, -jnp.inf))
        m_ref[h] = (cb_ref[h // heads_per_group] * dec).astype(BF16)

    first = lax.broadcasted_iota(jnp.int32, (1, LANES), 1) < SSM_HEAD_DIM
    for g in range(SSM_GROUPS):
        cols = slice(g * gw, (g + 1) * gw)
        x = xs_ref[:, cols]
        xb = x.astype(BF16)
        halves = []
        for pr in range(gw // LANES):
            h0 = g * heads_per_group + pr * HEADS_PER_VREG
            xp = xb[:, pr * LANES:(pr + 1) * LANES]
            halves.append(jnp.where(first, jnp.dot(m_ref[h0], xp, preferred_element_type=F32),
                                    jnp.dot(m_ref[h0 + 1], xp, preferred_element_type=F32)))
        y = jnp.concatenate(halves, axis=1)
        st = state_ref[g]
        y = y + jnp.dot(c_ref[:, g * D_STATE:(g + 1) * D_STATE], st.astype(BF16),
                        preferred_element_type=F32) * ex_ref[g, 0:q, :]
        y = y + x * dskip_ref[:, cols]
        g_ref[:, cols] = y * z_ref[:, cols].astype(F32)

    for g in range(SSM_GROUPS):
        cols = slice(g * gw, (g + 1) * gw)
        xw = (xs_ref[:, cols] * ex_ref[g, q:2 * q, :]).astype(BF16)
        upd = lax.dot_general(b_ref[:, g * D_STATE:(g + 1) * D_STATE], xw, TN_DIMS, preferred_element_type=F32)
        state_ref[g] = state_ref[g] * ex_ref[g, q - 1:q, :] + upd

    for g in range(SSM_GROUPS):
        cols = slice(g * gw, (g + 1) * gw)
        gated = g_ref[:, cols]
        ss = jnp.sum(gated * gated, axis=-1, keepdims=True)
        y_ref[:, cols] = (gated * lax.rsqrt(ss * (1.0 / gw) + RMS_EPS) * nw_ref[:, cols]).astype(BF16)


def _head_spread():
    k = jnp.arange(2 * LANES) % LANES
    j = jnp.arange(SSM_HEADS // SSM_GROUPS * SSM_HEAD_DIM) // SSM_HEAD_DIM
    g = jnp.arange(SSM_GROUPS) * (SSM_HEADS // SSM_GROUPS)
    return (k[None, :, None] == g[:, None, None] + j[None, None, :]).astype(BF16)


def _ssd_scan(xs, z, bm, cm, dt, dtT, alog, alogT, dskip, nw):
    L = xs.shape[0]
    q = SSD_CHUNK
    gn = SSM_GROUPS * D_STATE
    gw = SSM_HEADS // SSM_GROUPS * SSM_HEAD_DIM
    sel = _head_spread()
    return pl.pallas_call(
        _ssd_scan_kernel,
        grid=(L // q,),
        in_specs=[_rows(q, D_INNER), _rows(q, D_INNER), _rows(q, gn), _rows(q, gn), _rows(q, LANES),
                  pl.BlockSpec((SSM_HEADS, q), lambda i: (0, i)), _resident(alog.shape),
                  _resident(alogT.shape), _resident(dskip.shape), _resident(nw.shape), _resident(sel.shape)],
        out_specs=_rows(q, D_INNER),
        out_shape=jax.ShapeDtypeStruct((L, D_INNER), BF16),
        scratch_shapes=[pltpu.VMEM((SSM_GROUPS, D_STATE, gw), F32), pltpu.VMEM((SSM_GROUPS, q, q), F32),
                        pltpu.VMEM((SSM_HEADS, q, q), BF16), pltpu.VMEM((SSM_GROUPS, 2 * q, gw), F32),
                        pltpu.VMEM((q, D_INNER), F32)],
        compiler_params=_compiler_params(32 << 20),
        name="ssd_scan",
    )(xs, z, bm, cm, dt, dtT, alog, alogT, dskip, nw, sel)


def _post_kernel(m_ref, h_ref, wp_ref, g1_ref, b1_ref, w1_ref, w2_ref, g2_ref, b2_ref, out_ref, *, ff_chunk):
    mix = jnp.dot(m_ref[...], wp_ref[...], preferred_element_type=F32)
    h1 = _layer_norm(DEEPNORM_ALPHA * h_ref[...] + mix, g1_ref[...], b1_ref[...])
    hb = h1.astype(BF16)
    acc = None
    for c in range(D_FF // ff_chunk):
        cols = slice(c * ff_chunk, (c + 1) * ff_chunk)
        a = jnp.maximum(jnp.dot(hb, w1_ref[:, cols], preferred_element_type=F32), 0.0)
        t = jnp.dot((a * a).astype(BF16), w2_ref[cols, :], preferred_element_type=F32)
        acc = t if acc is None else acc + t
    out_ref[...] = _layer_norm(DEEPNORM_ALPHA * h1 + acc, g2_ref[...], b2_ref[...])


def _post(m, h, wp, wp_layer, g1, b1, w1, w2, layer, g2, b2, *, tm=512, ff_chunk=1024):
    L, kdim = m.shape
    vmem = (2 * (kdim * D_MODEL + 2 * D_MODEL * D_FF) + 2 * tm * (kdim * 2 + 2 * D_MODEL * 4)
            + 6 * tm * max(ff_chunk, D_MODEL) * 4 + (8 << 20))
    return pl.pallas_call(
        functools.partial(_post_kernel, ff_chunk=ff_chunk),
        grid=(L // tm,),
        in_specs=[_rows(tm, kdim), _rows(tm, D_MODEL), _layer_resident(wp, wp_layer), _resident(g1.shape),
                  _resident(b1.shape), _layer_resident(w1, layer), _layer_resident(w2, layer),
                  _resident(g2.shape), _resident(b2.shape)],
        out_specs=_rows(tm, D_MODEL),
        out_shape=jax.ShapeDtypeStruct((L, D_MODEL), F32),
        compiler_params=_compiler_params(vmem),
        name="post",
    )(m, h, wp, g1, b1, w1, w2, g2, b2)


def _kv_kernel(h_ref, wk_ref, wv_ref, k_ref, v_ref):
    hb = h_ref[...].astype(BF16)
    k_ref[...] = jnp.dot(hb, wk_ref[...], preferred_element_type=F32).astype(BF16)
    v_ref[...] = jnp.dot(hb, wv_ref[...], preferred_element_type=F32).astype(BF16)


def _kv(h, wk, wv, *, tm=512):
    L = h.shape[0]
    vmem = 2 * 2 * D_MODEL * D_MODEL + 2 * tm * D_MODEL * (4 + 2 + 2) + 4 * tm * D_MODEL * 4 + (8 << 20)
    return pl.pallas_call(
        _kv_kernel,
        grid=(L // tm,),
        in_specs=[_rows(tm, D_MODEL), _resident(wk.shape), _resident(wv.shape)],
        out_specs=[_rows(tm, D_MODEL), _rows(tm, D_MODEL)],
        out_shape=[jax.ShapeDtypeStruct((L, D_MODEL), BF16), jax.ShapeDtypeStruct((L, D_MODEL), BF16)],
        compiler_params=_compiler_params(vmem),
        name="kv",
    )(h, wk, wv)


def _attn_kernel(h_ref, wq_ref, tri_ref, k_hbm, v_hbm, o_ref, kbuf, vbuf, sem, q_ref, acc_ref, o_acc_ref,
                 split_ref, logit_ref, w_ref):
    tq = ATT_BLOCK
    n_pairs = SB_HEADS // HEADS_PER_VREG
    n_blocks = ATT_BLOCKS_PER_STEP * pl.num_programs(0)

    lane = lax.broadcasted_iota(jnp.int32, (1, LANES), 1)
    first = lane < SB_HEAD_DIM
    q = jnp.dot(h_ref[...].astype(BF16), wq_ref[...], preferred_element_type=F32) * (SB_HEAD_DIM ** -0.5 * LOG2_E)
    for b in range(ATT_BLOCKS_PER_STEP):
        for pair in range(n_pairs):
            qp = q[b * tq:(b + 1) * tq, pair * LANES:(pair + 1) * LANES]
            q_ref[b, 2 * pair * tq:(2 * pair + 1) * tq, :] = jnp.where(first, qp, 0.0).astype(BF16)
            q_ref[b, (2 * pair + 1) * tq:(2 * pair + 2) * tq, :] = jnp.where(first, 0.0, qp).astype(BF16)

    for b in range(ATT_BLOCKS_PER_STEP):
        _attn_block(ATT_BLOCKS_PER_STEP * pl.program_id(0) + b, n_blocks, first, q_ref.at[b], tri_ref, k_hbm,
                    v_hbm, o_ref.at[b * tq:(b + 1) * tq, :], kbuf, vbuf, sem, acc_ref, o_acc_ref, split_ref,
                    logit_ref, w_ref)


def _attn_block(i, n_blocks, first, q_ref, tri_ref, k_hbm, v_hbm, o_ref, kbuf, vbuf, sem, acc_ref, o_acc_ref,
                split_ref, logit_ref, w_ref):
    tq = ATT_BLOCK
    n_pairs = SB_HEADS // HEADS_PER_VREG

    def ring_slot(j):
        return lax.rem(j, ATT_RING)

    def demand_slot(j):
        return ATT_RING + lax.rem(j, 2)

    def copies(j, slot):
        rows = pl.ds(pl.multiple_of(j * tq, tq), tq)
        return (pltpu.make_async_copy(k_hbm.at[rows, :], kbuf.at[slot], sem.at[0, slot]),
                pltpu.make_async_copy(v_hbm.at[rows, :], vbuf.at[slot], sem.at[1, slot]))

    def fetch(j, slot):
        for cp in copies(j, slot):
            cp.start()

    def wait(j, slot):
        for cp in copies(j, slot):
            cp.wait()

    @pl.when(i == 0)
    def _():
        fetch(i, ring_slot(i))

    @pl.when(i + 1 < n_blocks)
    def _():
        fetch(i + 1, ring_slot(i + 1))

    acc_ref[...] = jnp.zeros_like(acc_ref)
    o_acc_ref[...] = jnp.zeros_like(o_acc_ref)

    def pair_cols(pair):
        return slice(pair * LANES, (pair + 1) * LANES)

    def process(slot, diag, nrows):
        def load(ref, pair):
            base = 2 * pair * tq
            return jnp.concatenate([ref[base:base + nrows, :], ref[base + tq:base + tq + nrows, :]], axis=0)

        def store(ref, pair, val):
            base = 2 * pair * tq
            ref[base:base + nrows, :] = val[:nrows]
            ref[base + tq:base + tq + nrows, :] = val[nrows:]

        if diag:
            ri = lax.broadcasted_iota(jnp.int32, (2 * nrows, tq), 0) % nrows
            ci = lax.broadcasted_iota(jnp.int32, (2 * nrows, tq), 1)
            strictly_before = ci < ri

        for pair in range(n_pairs):
            z = lax.dot_general(load(q_ref, pair), kbuf[slot, :, pair_cols(pair)], NT_DIMS,
                                preferred_element_type=F32)
            sp = jnp.maximum(z, 0.0) + jnp.log2(1.0 + jnp.exp2(-jnp.abs(z)))
            if diag:
                sp = jnp.where(strictly_before, sp, 0.0)
            hi = sp.astype(BF16)
            store(split_ref, pair, jnp.concatenate([hi, (sp - hi.astype(F32)).astype(BF16)], axis=1))
            store(logit_ref, pair, z - sp)
        live_top = None
        live_rest = None
        for pair in range(n_pairs):
            r = jnp.dot(load(split_ref, pair), tri_ref[...], preferred_element_type=F32)
            acc = load(acc_ref, pair)
            w = jnp.exp2(load(logit_ref, pair) - r[:, :LANES] + acc)
            if diag:
                w = jnp.where(strictly_before, w, 0.0)
            store(w_ref, pair, w.astype(BF16))
            acc = acc - r[:, LANES:]
            store(acc_ref, pair, acc)
            top = jnp.maximum(acc[:ATT_TOP_ROWS], acc[nrows:nrows + ATT_TOP_ROWS])
            live_top = top if live_top is None else jnp.maximum(live_top, top)
            if nrows > ATT_TOP_ROWS:
                rest = jnp.maximum(acc[ATT_TOP_ROWS:nrows], acc[nrows + ATT_TOP_ROWS:])
                live_rest = rest if live_rest is None else jnp.maximum(live_rest, rest)
        go_top = jnp.max(live_top) >= LOG2_F32_UNDERFLOW
        go_rest = (jnp.max(live_rest) >= LOG2_F32_UNDERFLOW) if nrows > ATT_TOP_ROWS else jnp.bool_(False)
        for pair in range(n_pairs):
            cols = pair_cols(pair)
            pv = jnp.dot(load(w_ref, pair), vbuf[slot, :, cols], preferred_element_type=F32)
            o_acc_ref[0:nrows, cols] += jnp.where(first, pv[:nrows], pv[nrows:])
        return jnp.logical_or(go_top, go_rest).astype(jnp.int32), go_rest.astype(jnp.int32)

    wait(i, ring_slot(i))
    flags_after_diag = process(ring_slot(i), True, tq)

    def on_demand(j):
        return jnp.logical_and(j >= 0, j <= i - (ATT_RING - 1))

    def in_flight(j):
        return jnp.logical_and(on_demand(j), on_demand(j + 1))

    def cond(carry):
        j, go, _ = carry
        return jnp.logical_and(j >= 0, go > 0)

    def body(carry):
        j, _, go_rest = carry
        demand = on_demand(j)
        slot = jnp.where(demand, demand_slot(j), ring_slot(j))

        @pl.when(jnp.logical_and(demand, jnp.logical_not(in_flight(j))))
        def _():
            fetch(j, slot)

        @pl.when(demand)
        def _():
            wait(j, slot)

        @pl.when(in_flight(j - 1))
        def _():
            fetch(j - 1, demand_slot(j - 1))

        go, go_rest = lax.cond(go_rest > 0,
                               lambda: process(slot, False, tq),
                               lambda: process(slot, False, ATT_TOP_ROWS))
        return j - 1, go, go_rest

    j_end, _, _ = lax.while_loop(cond, body, (i - 1,) + flags_after_diag)

    @pl.when(in_flight(j_end))
    def _():
        wait(j_end, demand_slot(j_end))

    o_ref[...] = o_acc_ref[...].astype(BF16)


def _attn_tri():
    j = jnp.arange(ATT_BLOCK)
    later = (j[:, None] > j[None, :]).astype(BF16)
    half = jnp.concatenate([later, jnp.ones((ATT_BLOCK, LANES), BF16)], axis=1)
    return jnp.concatenate([half, half], axis=0)


def _attn(h, wq, wq_layer, k, v):
    L = h.shape[0]
    tq = ATT_BLOCK
    n_pairs = SB_HEADS // HEADS_PER_VREG
    tri = _attn_tri()
    rows = ATT_BLOCKS_PER_STEP * tq
    return pl.pallas_call(
        _attn_kernel,
        grid=(L // rows,),
        in_specs=[_rows(rows, D_MODEL), _layer_resident(wq, wq_layer), _resident(tri.shape),
                  pl.BlockSpec(memory_space=pl.ANY), pl.BlockSpec(memory_space=pl.ANY)],
        out_specs=_rows(rows, D_MODEL),
        out_shape=jax.ShapeDtypeStruct((L, D_MODEL), BF16),
        scratch_shapes=[pltpu.VMEM((ATT_RING + 2, tq, D_MODEL), BF16),
                        pltpu.VMEM((ATT_RING + 2, tq, D_MODEL), BF16),
                        pltpu.SemaphoreType.DMA((2, ATT_RING + 2)),
                        pltpu.VMEM((ATT_BLOCKS_PER_STEP, 2 * n_pairs * tq, LANES), BF16),
                        pltpu.VMEM((2 * n_pairs * tq, LANES), F32), pltpu.VMEM((tq, D_MODEL), F32),
                        pltpu.VMEM((2 * n_pairs * tq, 2 * LANES), BF16),
                        pltpu.VMEM((2 * n_pairs * tq, LANES), F32),
                        pltpu.VMEM((2 * n_pairs * tq, LANES), BF16)],
        compiler_params=_compiler_params(32 << 20),
        name="attn",
    )(h, wq, tri, k, v)


def _row(v):
    return v.reshape(1, -1).astype(F32)


def _pad_lanes(v, width=LANES):
    return jnp.pad(v, [(0, 0)] * (v.ndim - 1) + [(0, width - v.shape[-1])])


def kernel(x, ssm_w_in, ssm_conv_w, ssm_conv_b, ssm_dt_bias, ssm_a_log, ssm_d, ssm_norm_w, ssm_w_out,
           sb_w_k, sb_w_v, sb_w_q, sb_w_o, mlp_w1, mlp_w2, ln_mix_g, ln_mix_b, ln_mlp_g, ln_mlp_b):
    assert x.shape[0] == 1 and x.shape[2] == D_MODEL
    L = x.shape[1]
    assert L % 512 == 0
    h = x.reshape(L, D_MODEL)
    w_in = ssm_w_in.astype(BF16)
    w_out = ssm_w_out.astype(BF16)
    w_q = sb_w_q.astype(BF16)
    w_o = sb_w_o.astype(BF16)
    w1 = mlp_w1.astype(BF16)
    w2 = mlp_w2.astype(BF16)
    k = v = None
    for layer in range(DEPTH):
        if layer < N_SSD_LAYERS:
            w_dt = ssm_w_in[layer, :, D_INNER + CONV_DIM:]
            wdt = _pad_lanes(w_dt).astype(BF16)
            wdtT = w_dt.T.astype(BF16)
            dtb = _pad_lanes(_row(ssm_dt_bias[layer]))
            dtbT = ssm_dt_bias[layer].reshape(-1, 1).astype(F32)
            z, xs, bm, cm, dt, dtT = _ssd_in(h, w_in, layer, wdt, wdtT, ssm_conv_w,
                                             _row(ssm_conv_b[layer]), dtb, dtbT)
            alog = _pad_lanes(_row(ssm_a_log[layer]))
            alogT = ssm_a_log[layer].reshape(-1, 1).astype(F32)
            dskip = _row(jnp.repeat(ssm_d[layer], SSM_HEAD_DIM))
            mix_in = _ssd_scan(xs, z, bm, cm, dt, dtT, alog, alogT, dskip, _row(ssm_norm_w[layer]))
            w_proj, proj_layer = w_out, layer
        else:
            if layer == N_SSD_LAYERS:
                k, v = _kv(h, sb_w_k.astype(BF16), sb_w_v.astype(BF16))
            j = layer - N_SSD_LAYERS
            mix_in = _attn(h, w_q, j, k, v)
            w_proj, proj_layer = w_o, j
        h = _post(mix_in, h, w_proj, proj_layer, _row(ln_mix_g[layer]), _row(ln_mix_b[layer]),
                  w1, w2, layer, _row(ln_mlp_g[layer]), _row(ln_mlp_b[layer]))
    return h.reshape(1, L, D_MODEL)
```

```python
import functools

import jax
import jax.numpy as jnp
from jax import lax
from jax.experimental import pallas as pl
from jax.experimental.pallas import tpu as pltpu

F32 = jnp.float32
BF16 = jnp.bfloat16

D_MODEL = 1024
DEPTH = 4
N_SSD_LAYERS = DEPTH // 2
D_INNER = 2 * D_MODEL
SSM_HEADS = 32
SSM_HEAD_DIM = 64
SSM_GROUPS = 8
D_STATE = 128
D_CONV = 4
CONV_DIM = D_INNER + 2 * SSM_GROUPS * D_STATE
SB_HEADS = 16
SB_HEAD_DIM = 64
D_FF = 4 * D_MODEL
DEEPNORM_ALPHA = (2 * DEPTH) ** 0.25
LN_EPS = 1e-5
RMS_EPS = 1e-5

LANES = 128
SUBLANES = 8
HEADS_PER_VREG = LANES // SSM_HEAD_DIM
CONV_PHASES = 4
SSD_CHUNK = 128
ATT_BLOCK = 128
ATT_BLOCKS_PER_STEP = 4
ATT_RING = 4
ATT_TOP_ROWS = 64
LOG2_E = 1.4426950408889634
LOG2_F32_UNDERFLOW = -150.0
VMEM_LIMIT_CAP = 58 * 1024 * 1024

NT_DIMS = (((1,), (1,)), ((), ()))
TN_DIMS = (((0,), (0,)), ((), ()))


def _softplus(x):
    return jnp.maximum(x, 0.0) + jnp.log1p(jnp.exp(-jnp.abs(x)))


def _silu(x):
    half = 0.5 * x
    return half + half * jnp.tanh(half)


def _layer_norm(x, g, b):
    mu = jnp.mean(x, axis=-1, keepdims=True)
    xc = x - mu
    var = jnp.mean(xc * xc, axis=-1, keepdims=True)
    return xc * lax.rsqrt(var + LN_EPS) * g + b


def _split_bf16(x, parts):
    out = []
    r = x
    for _ in range(parts - 1):
        p = r.astype(BF16)
        out.append(p)
        r = r - p.astype(F32)
    out.append(r.astype(BF16))
    return out


def _dot01_left(m01, x, parts=3):
    acc = None
    for p in _split_bf16(x, parts):
        t = jnp.dot(m01, p, preferred_element_type=F32)
        acc = t if acc is None else acc + t
    return acc


def _dot01_right(x, m01, parts=3):
    acc = None
    for p in _split_bf16(x, parts):
        t = jnp.dot(p, m01, preferred_element_type=F32)
        acc = t if acc is None else acc + t
    return acc


def _resident(shape):
    nd = len(shape)
    return pl.BlockSpec(shape, lambda i: (0,) * nd, pipeline_mode=pl.Buffered(1))


def _layer_resident(stacked, layer):
    nd = stacked.ndim - 1
    return pl.BlockSpec((None,) + stacked.shape[1:], lambda i: (layer,) + (0,) * nd,
                        pipeline_mode=pl.Buffered(1))


def _rows(tm, width):
    return pl.BlockSpec((tm, width), lambda i: (i, 0))


def _compiler_params(vmem_bytes, semantics=("arbitrary",)):
    return pltpu.CompilerParams(dimension_semantics=semantics,
                                vmem_limit_bytes=min(int(vmem_bytes), VMEM_LIMIT_CAP))


def _ssd_in_kernel(h_ref, win_ref, wdt_ref, wdtT_ref, cw_ref, cb_ref, dtb_ref, dtbT_ref,
                   z_ref, xs_ref, b_ref, c_ref, dt_ref, dtT_ref, carry_ref, work_ref, y_ref, *, tm, cchunk):
    hb = h_ref[...].astype(BF16)
    dt_ref[...] = _softplus(jnp.dot(hb, wdt_ref[...], preferred_element_type=F32) + dtb_ref[...])
    dtT_ref[...] = _softplus(
        lax.dot_general(wdtT_ref[...], hb, NT_DIMS, preferred_element_type=F32) + dtbT_ref[...])

    @pl.when(pl.program_id(0) == 0)
    def _():
        carry_ref[...] = jnp.zeros_like(carry_ref)

    n_chunks = CONV_DIM // cchunk
    z_chunk = D_INNER // n_chunks
    slabs = cchunk // LANES

    def project(c):
        slot = c % 2
        res = jnp.dot(hb, win_ref[:, D_INNER + c * cchunk:D_INNER + (c + 1) * cchunk],
                      preferred_element_type=F32)
        for s in range(slabs):
            slab = c * slabs + s
            work_ref[slot, s, 0:SUBLANES, :] = carry_ref[slab]
            work_ref[slot, s, SUBLANES:SUBLANES + tm, :] = res[:, s * LANES:(s + 1) * LANES]
            carry_ref[slab] = work_ref[slot, s, tm:tm + SUBLANES, :]

    def conv(c):
        slot = c % 2
        for s in range(slabs):
            lo = c * cchunk + s * LANES
            lanes = slice(lo, lo + LANES)
            for phase in range(CONV_PHASES):
                acc = cb_ref[:, lanes]
                for k in range(D_CONV):
                    start = SUBLANES + phase - (D_CONV - 1) + k
                    u = work_ref[slot, s, pl.ds(start, tm // CONV_PHASES, stride=CONV_PHASES), :]
                    acc = acc + cw_ref[k:k + 1, lanes] * u
                y_ref[s, pl.ds(phase, tm // CONV_PHASES, stride=CONV_PHASES), :] = _silu(acc)
            y = y_ref[s]
            if lo < D_INNER:
                xs_ref[:, lanes] = y
            elif lo < D_INNER + SSM_GROUPS * D_STATE:
                b_ref[:, lo - D_INNER:lo - D_INNER + LANES] = y.astype(BF16)
            else:
                off = lo - D_INNER - SSM_GROUPS * D_STATE
                c_ref[:, off:off + LANES] = y.astype(BF16)

    project(0)
    for c in range(n_chunks):
        if c + 1 < n_chunks:
            project(c + 1)
        zc = slice(c * z_chunk, (c + 1) * z_chunk)
        z_ref[:, zc] = _silu(jnp.dot(hb, win_ref[:, zc], preferred_element_type=F32)).astype(BF16)
        conv(c)


def _ssd_in(h, w_in, layer, wdt, wdtT, conv_w, conv_b, dtb, dtbT, *, tm=512, cchunk=512):
    L = h.shape[0]
    gn = SSM_GROUPS * D_STATE
    vmem = (2 * (D_MODEL * (D_INNER + CONV_DIM + LANES)) + 2 * tm * D_MODEL * 4
            + 2 * tm * (2 * D_INNER * 4 + 2 * gn * 2 + LANES * 4 + SSM_HEADS * 4)
            + 2 * (tm + SUBLANES) * cchunk * 4 + 6 * tm * cchunk * 4 + (8 << 20))
    return pl.pallas_call(
        functools.partial(_ssd_in_kernel, tm=tm, cchunk=cchunk),
        grid=(L // tm,),
        in_specs=[_rows(tm, D_MODEL), _layer_resident(w_in, layer), _resident(wdt.shape),
                  _resident(wdtT.shape), _layer_resident(conv_w, layer), _resident(conv_b.shape),
                  _resident(dtb.shape), _resident(dtbT.shape)],
        out_specs=[_rows(tm, D_INNER), _rows(tm, D_INNER), _rows(tm, gn), _rows(tm, gn),
                   _rows(tm, LANES), pl.BlockSpec((SSM_HEADS, tm), lambda i: (0, i))],
        out_shape=[jax.ShapeDtypeStruct((L, D_INNER), BF16), jax.ShapeDtypeStruct((L, D_INNER), F32),
                   jax.ShapeDtypeStruct((L, gn), BF16), jax.ShapeDtypeStruct((L, gn), BF16),
                   jax.ShapeDtypeStruct((L, LANES), F32), jax.ShapeDtypeStruct((SSM_HEADS, L), F32)],
        scratch_shapes=[pltpu.VMEM((CONV_DIM // LANES, SUBLANES, LANES), F32),
                        pltpu.VMEM((2, cchunk // LANES, tm + SUBLANES, LANES), F32),
                        pltpu.VMEM((cchunk // LANES, tm, LANES), F32)],
        compiler_params=_compiler_params(vmem),
        name="ssd_in",
    )(h, w_in, wdt, wdtT, conv_w, conv_b, dtb, dtbT)


def _ssd_scan_kernel(xs_ref, z_ref, b_ref, c_ref, dt_ref, dtT_ref, alog_ref, alogT_ref, dskip_ref,
                     nw_ref, sel_ref, y_ref, state_ref, cb_ref, m_ref, ex_ref, g_ref):
    q = SSD_CHUNK
    heads_per_group = SSM_HEADS // SSM_GROUPS
    gw = heads_per_group * SSM_HEAD_DIM

    @pl.when(pl.program_id(0) == 0)
    def _():
        state_ref[...] = jnp.zeros_like(state_ref)

    dt = dt_ref[...]
    dtT = dtT_ref[...]
    da = dt * (-LOG2_E * jnp.exp(alog_ref[...]))
    daT = dtT * (-LOG2_E * jnp.exp(alogT_ref[...]))
    ri = lax.broadcasted_iota(jnp.int32, (q, q), 0)
    ci = lax.broadcasted_iota(jnp.int32, (q, q), 1)
    tril = ci <= ri
    ltri = jnp.where(tril, 1.0, 0.0).astype(BF16)
    utri = jnp.where(ri <= ci, 1.0, 0.0).astype(BF16)
    cs = _dot01_left(ltri, da)
    csT = _dot01_right(daT, utri) - jnp.log2(dtT)
    cs_last = cs[q - 1:q, :]
    per_head = jnp.concatenate([jnp.exp2(cs), jnp.exp2(cs_last - cs) * dt], axis=0)
    hi = per_head.astype(BF16)
    per_head_split = jnp.concatenate([hi, (per_head - hi.astype(F32)).astype(BF16)], axis=1)

    for g in range(SSM_GROUPS):
        gcols = slice(g * D_STATE, (g + 1) * D_STATE)
        ex_ref[g] = jnp.dot(per_head_split, sel_ref[g], preferred_element_type=F32)
        cb_ref[g] = lax.dot_general(c_ref[:, gcols], b_ref[:, gcols], NT_DIMS, preferred_element_type=F32)

    for h in range(SSM_HEADS):
        diff = cs[:, h:h + 1] - csT[h:h + 1, :]
        dec = jnp.exp2(jnp.where(tril, diff, -jnp.inf))
        m_ref[h] = (cb_ref[h // heads_per_group] * dec).astype(BF16)

    first = lax.broadcasted_iota(jnp.int32, (1, LANES), 1) < SSM_HEAD_DIM
    for g in range(SSM_GROUPS):
        cols = slice(g * gw, (g + 1) * gw)
        x = xs_ref[:, cols]
        xb = x.astype(BF16)
        halves = []
        for pr in range(gw // LANES):
            h0 = g * heads_per_group + pr * HEADS_PER_VREG
            xp = xb[:, pr * LANES:(pr + 1) * LANES]
            halves.append(jnp.where(first, jnp.dot(m_ref[h0], xp, preferred_element_type=F32),
                                    jnp.dot(m_ref[h0 + 1], xp, preferred_element_type=F32)))
        y = jnp.concatenate(halves, axis=1)
        st = state_ref[g]
        y = y + jnp.dot(c_ref[:, g * D_STATE:(g + 1) * D_STATE], st.astype(BF16),
                        preferred_element_type=F32) * ex_ref[g, 0:q, :]
        y = y + x * dskip_ref[:, cols]
        g_ref[:, cols] = y * z_ref[:, cols].astype(F32)

    for g in range(SSM_GROUPS):
        cols = slice(g * gw, (g + 1) * gw)
        xw = (xs_ref[:, cols] * ex_ref[g, q:2 * q, :]).astype(BF16)
        upd = lax.dot_general(b_ref[:, g * D_STATE:(g + 1) * D_STATE], xw, TN_DIMS, preferred_element_type=F32)
        state_ref[g] = state_ref[g] * ex_ref[g, q - 1:q, :] + upd

    for g in range(SSM_GROUPS):
        cols = slice(g * gw, (g + 1) * gw)
        gated = g_ref[:, cols]
        ss = jnp.sum(gated * gated, axis=-1, keepdims=True)
        y_ref[:, cols] = (gated * lax.rsqrt(ss * (1.0 / gw) + RMS_EPS) * nw_ref[:, cols]).astype(BF16)


def _head_spread():
    k = jnp.arange(2 * LANES) % LANES
    j = jnp.arange(SSM_HEADS // SSM_GROUPS * SSM_HEAD_DIM) // SSM_HEAD_DIM
    g = jnp.arange(SSM_GROUPS) * (SSM_HEADS // SSM_GROUPS)
    return (k[None, :, None] == g[:, None, None] + j[None, None, :]).astype(BF16)


def _ssd_scan(xs, z, bm, cm, dt, dtT, alog, alogT, dskip, nw):
    L = xs.shape[0]
    q = SSD_CHUNK
    gn = SSM_GROUPS * D_STATE
    gw = SSM_HEADS // SSM_GROUPS * SSM_HEAD_DIM
    sel = _head_spread()
    return pl.pallas_call(
        _ssd_scan_kernel,
        grid=(L // q,),
        in_specs=[_rows(q, D_INNER), _rows(q, D_INNER), _rows(q, gn), _rows(q, gn), _rows(q, LANES),
                  pl.BlockSpec((SSM_HEADS, q), lambda i: (0, i)), _resident(alog.shape),
                  _resident(alogT.shape), _resident(dskip.shape), _resident(nw.shape), _resident(sel.shape)],
        out_specs=_rows(q, D_INNER),
        out_shape=jax.ShapeDtypeStruct((L, D_INNER), BF16),
        scratch_shapes=[pltpu.VMEM((SSM_GROUPS, D_STATE, gw), F32), pltpu.VMEM((SSM_GROUPS, q, q), F32),
                        pltpu.VMEM((SSM_HEADS, q, q), BF16), pltpu.VMEM((SSM_GROUPS, 2 * q, gw), F32),
                        pltpu.VMEM((q, D_INNER), F32)],
        compiler_params=_compiler_params(32 << 20),
        name="ssd_scan",
    )(xs, z, bm, cm, dt, dtT, alog, alogT, dskip, nw, sel)


def _post_kernel(m_ref, h_ref, wp_ref, g1_ref, b1_ref, w1_ref, w2_ref, g2_ref, b2_ref, out_ref, *, ff_chunk):
    mix = jnp.dot(m_ref[...], wp_ref[...], preferred_element_type=F32)
    h1 = _layer_norm(DEEPNORM_ALPHA * h_ref[...] + mix, g1_ref[...], b1_ref[...])
    hb = h1.astype(BF16)
    acc = None
    for c in range(D_FF // ff_chunk):
        cols = slice(c * ff_chunk, (c + 1) * ff_chunk)
        a = jnp.maximum(jnp.dot(hb, w1_ref[:, cols], preferred_element_type=F32), 0.0)
        t = jnp.dot((a * a).astype(BF16), w2_ref[cols, :], preferred_element_type=F32)
        acc = t if acc is None else acc + t
    out_ref[...] = _layer_norm(DEEPNORM_ALPHA * h1 + acc, g2_ref[...], b2_ref[...])


def _post(m, h, wp, wp_layer, g1, b1, w1, w2, layer, g2, b2, *, tm=512, ff_chunk=1024):
    L, kdim = m.shape
    vmem = (2 * (kdim * D_MODEL + 2 * D_MODEL * D_FF) + 2 * tm * (kdim * 2 + 2 * D_MODEL * 4)
            + 6 * tm * max(ff_chunk, D_MODEL) * 4 + (8 << 20))
    return pl.pallas_call(
        functools.partial(_post_kernel, ff_chunk=ff_chunk),
        grid=(L // tm,),
        in_specs=[_rows(tm, kdim), _rows(tm, D_MODEL), _layer_resident(wp, wp_layer), _resident(g1.shape),
                  _resident(b1.shape), _layer_resident(w1, layer), _layer_resident(w2, layer),
                  _resident(g2.shape), _resident(b2.shape)],
        out_specs=_rows(tm, D_MODEL),
        out_shape=jax.ShapeDtypeStruct((L, D_MODEL), F32),
        compiler_params=_compiler_params(vmem),
        name="post",
    )(m, h, wp, g1, b1, w1, w2, g2, b2)


def _kv_kernel(h_ref, wk_ref, wv_ref, k_ref, v_ref):
    hb = h_ref[...].astype(BF16)
    k_ref[...] = jnp.dot(hb, wk_ref[...], preferred_element_type=F32).astype(BF16)
    v_ref[...] = jnp.dot(hb, wv_ref[...], preferred_element_type=F32).astype(BF16)


def _kv(h, wk, wv, *, tm=512):
    L = h.shape[0]
    vmem = 2 * 2 * D_MODEL * D_MODEL + 2 * tm * D_MODEL * (4 + 2 + 2) + 4 * tm * D_MODEL * 4 + (8 << 20)
    return pl.pallas_call(
        _kv_kernel,
        grid=(L // tm,),
        in_specs=[_rows(tm, D_MODEL), _resident(wk.shape), _resident(wv.shape)],
        out_specs=[_rows(tm, D_MODEL), _rows(tm, D_MODEL)],
        out_shape=[jax.ShapeDtypeStruct((L, D_MODEL), BF16), jax.ShapeDtypeStruct((L, D_MODEL), BF16)],
        compiler_params=_compiler_params(vmem),
        name="kv",
    )(h, wk, wv)


def _attn_kernel(h_ref, wq_ref, tri_ref, k_hbm, v_hbm, o_ref, kbuf, vbuf, sem, q_ref, acc_ref, o_acc_ref,
                 split_ref, logit_ref, w_ref):
    tq = ATT_BLOCK
    n_pairs = SB_HEADS // HEADS_PER_VREG
    n_blocks = ATT_BLOCKS_PER_STEP * pl.num_programs(0)

    lane = lax.broadcasted_iota(jnp.int32, (1, LANES), 1)
    first = lane < SB_HEAD_DIM
    q = jnp.dot(h_ref[...].astype(BF16), wq_ref[...], preferred_element_type=F32) * (SB_HEAD_DIM ** -0.5 * LOG2_E)
    for b in range(ATT_BLOCKS_PER_STEP):
        for pair in range(n_pairs):
            qp = q[b * tq:(b + 1) * tq, pair * LANES:(pair + 1) * LANES]
            q_ref[b, 2 * pair * tq:(2 * pair + 1) * tq, :] = jnp.where(first, qp, 0.0).astype(BF16)
            q_ref[b, (2 * pair + 1) * tq:(2 * pair + 2) * tq, :] = jnp.where(first, 0.0, qp).astype(BF16)

    for b in range(ATT_BLOCKS_PER_STEP):
        _attn_block(ATT_BLOCKS_PER_STEP * pl.program_id(0) + b, n_blocks, first, q_ref.at[b], tri_ref, k_hbm,
                    v_hbm, o_ref.at[b * tq:(b + 1) * tq, :], kbuf, vbuf, sem, acc_ref, o_acc_ref, split_ref,
                    logit_ref, w_ref)


def _attn_block(i, n_blocks, first, q_ref, tri_ref, k_hbm, v_hbm, o_ref, kbuf, vbuf, sem, acc_ref, o_acc_ref,
                split_ref, logit_ref, w_ref):
    tq = ATT_BLOCK
    n_pairs = SB_HEADS // HEADS_PER_VREG

    def ring_slot(j):
        return lax.rem(j, ATT_RING)

    def demand_slot(j):
        return ATT_RING + lax.rem(j, 2)

    def copies(j, slot):
        rows = pl.ds(pl.multiple_of(j * tq, tq), tq)
        return (pltpu.make_async_copy(k_hbm.at[rows, :], kbuf.at[slot], sem.at[0, slot]),
                pltpu.make_async_copy(v_hbm.at[rows, :], vbuf.at[slot], sem.at[1, slot]))

    def fetch(j, slot):
        for cp in copies(j, slot):
            cp.start()

    def wait(j, slot):
        for cp in copies(j, slot):
            cp.wait()

    @pl.when(i == 0)
    def _():
        fetch(i, ring_slot(i))

    @pl.when(i + 1 < n_blocks)
    def _():
        fetch(i + 1, ring_slot(i + 1))

    acc_ref[...] = jnp.zeros_like(acc_ref)
    o_acc_ref[...] = jnp.zeros_like(o_acc_ref)

    def pair_cols(pair):
        return slice(pair * LANES, (pair + 1) * LANES)

    def process(slot, diag, nrows):
        def load(ref, pair):
            base = 2 * pair * tq
            return jnp.concatenate([ref[base:base + nrows, :], ref[base + tq:base + tq + nrows, :]], axis=0)

        def store(ref, pair, val):
            base = 2 * pair * tq
            ref[base:base + nrows, :] = val[:nrows]
            ref[base + tq:base + tq + nrows, :] = val[nrows:]

        if diag:
            ri = lax.broadcasted_iota(jnp.int32, (2 * nrows, tq), 0) % nrows
            ci = lax.broadcasted_iota(jnp.int32, (2 * nrows, tq), 1)
            strictly_before = ci < ri

        for pair in range(n_pairs):
            z = lax.dot_general(load(q_ref, pair), kbuf[slot, :, pair_cols(pair)], NT_DIMS,
                                preferred_element_type=F32)
            sp = jnp.maximum(z, 0.0) + jnp.log2(1.0 + jnp.exp2(-jnp.abs(z)))
            if diag:
                sp = jnp.where(strictly_before, sp, 0.0)
            hi = sp.astype(BF16)
            store(split_ref, pair, jnp.concatenate([hi, (sp - hi.astype(F32)).astype(BF16)], axis=1))
            store(logit_ref, pair, z - sp)
        live_top = None
        live_rest = None
        for pair in range(n_pairs):
            r = jnp.dot(load(split_ref, pair), tri_ref[...], preferred_element_type=F32)
            acc = load(acc_ref, pair)
            w = jnp.exp2(load(logit_ref, pair) - r[:, :LANES] + acc)
            if diag:
                w = jnp.where(strictly_before, w, 0.0)
            store(w_ref, pair, w.astype(BF16))
            acc = acc - r[:, LANES:]
            store(acc_ref, pair, acc)
            top = jnp.maximum(acc[:ATT_TOP_ROWS], acc[nrows:nrows + ATT_TOP_ROWS])
            live_top = top if live_top is None else jnp.maximum(live_top, top)
            if nrows > ATT_TOP_ROWS:
                rest = jnp.maximum(acc[ATT_TOP_ROWS:nrows], acc[nrows + ATT_TOP_ROWS:])
                live_rest = rest if live_rest is None else jnp.maximum(live_rest, rest)
        go_top = jnp.max(live_top) >= LOG2_F32_UNDERFLOW
        go_rest = (jnp.max(live_rest) >= LOG2_F32_UNDERFLOW) if nrows > ATT_TOP_ROWS else jnp.bool_(False)
        for pair in range(n_pairs):
            cols = pair_cols(pair)
            pv = jnp.dot(load(w_ref, pair), vbuf[slot, :, cols], preferred_element_type=F32)
            o_acc_ref[0:nrows, cols] += jnp.where(first, pv[:nrows], pv[nrows:])
        return jnp.logical_or(go_top, go_rest).astype(jnp.int32), go_rest.astype(jnp.int32)

    wait(i, ring_slot(i))
    flags_after_diag = process(ring_slot(i), True, tq)

    def on_demand(j):
        return jnp.logical_and(j >= 0, j <= i - (ATT_RING - 1))

    def in_flight(j):
        return jnp.logical_and(on_demand(j), on_demand(j + 1))

    def cond(carry):
        j, go, _ = carry
        return jnp.logical_and(j >= 0, go > 0)

    def body(carry):
        j, _, go_rest = carry
        demand = on_demand(j)
        slot = jnp.where(demand, demand_slot(j), ring_slot(j))

        @pl.when(jnp.logical_and(demand, jnp.logical_not(in_flight(j))))
        def _():
            fetch(j, slot)

        @pl.when(demand)
        def _():
            wait(j, slot)

        @pl.when(in_flight(j - 1))
        def _():
            fetch(j - 1, demand_slot(j - 1))

        go, go_rest = lax.cond(go_rest > 0,
                               lambda: process(slot, False, tq),
                               lambda: process(slot, False, ATT_TOP_ROWS))
        return j - 1, go, go_rest

    j_end, _, _ = lax.while_loop(cond, body, (i - 1,) + flags_after_diag)

    @pl.when(in_flight(j_end))
    def _():
        wait(j_end, demand_slot(j_end))

    o_ref[...] = o_acc_ref[...].astype(BF16)


def _attn_tri():
    j = jnp.arange(ATT_BLOCK)
    later = (j[:, None] > j[None, :]).astype(BF16)
    half = jnp.concatenate([later, jnp.ones((ATT_BLOCK, LANES), BF16)], axis=1)
    return jnp.concatenate([half, half], axis=0)


def _attn(h, wq, wq_layer, k, v):
    L = h.shape[0]
    tq = ATT_BLOCK
    n_pairs = SB_HEADS // HEADS_PER_VREG
    tri = _attn_tri()
    rows = ATT_BLOCKS_PER_STEP * tq
    return pl.pallas_call(
        _attn_kernel,
        grid=(L // rows,),
        in_specs=[_rows(rows, D_MODEL), _layer_resident(wq, wq_layer), _resident(tri.shape),
                  pl.BlockSpec(memory_space=pl.ANY), pl.BlockSpec(memory_space=pl.ANY)],
        out_specs=_rows(rows, D_MODEL),
        out_shape=jax.ShapeDtypeStruct((L, D_MODEL), BF16),
        scratch_shapes=[pltpu.VMEM((ATT_RING + 2, tq, D_MODEL), BF16),
                        pltpu.VMEM((ATT_RING + 2, tq, D_MODEL), BF16),
                        pltpu.SemaphoreType.DMA((2, ATT_RING + 2)),
                        pltpu.VMEM((ATT_BLOCKS_PER_STEP, 2 * n_pairs * tq, LANES), BF16),
                        pltpu.VMEM((2 * n_pairs * tq, LANES), F32), pltpu.VMEM((tq, D_MODEL), F32),
                        pltpu.VMEM((2 * n_pairs * tq, 2 * LANES), BF16),
                        pltpu.VMEM((2 * n_pairs * tq, LANES), F32),
                        pltpu.VMEM((2 * n_pairs * tq, LANES), BF16)],
        compiler_params=_compiler_params(32 << 20),
        name="attn",
    )(h, wq, tri, k, v)


def _row(v):
    return v.reshape(1, -1).astype(F32)


def _pad_lanes(v, width=LANES):
    return jnp.pad(v, [(0, 0)] * (v.ndim - 1) + [(0, width - v.shape[-1])])


def kernel(x, ssm_w_in, ssm_conv_w, ssm_conv_b, ssm_dt_bias, ssm_a_log, ssm_d, ssm_norm_w, ssm_w_out,
           sb_w_k, sb_w_v, sb_w_q, sb_w_o, mlp_w1, mlp_w2, ln_mix_g, ln_mix_b, ln_mlp_g, ln_mlp_b):
    assert x.shape[0] == 1 and x.shape[2] == D_MODEL
    L = x.shape[1]
    assert L % 512 == 0
    h = x.reshape(L, D_MODEL)
    w_in = ssm_w_in.astype(BF16)
    w_out = ssm_w_out.astype(BF16)
    w_q = sb_w_q.astype(BF16)
    w_o = sb_w_o.astype(BF16)
    w1 = mlp_w1.astype(BF16)
    w2 = mlp_w2.astype(BF16)
    k = v = None
    for layer in range(DEPTH):
        if layer < N_SSD_LAYERS:
            w_dt = ssm_w_in[layer, :, D_INNER + CONV_DIM:]
            wdt = _pad_lanes(w_dt).astype(BF16)
            wdtT = w_dt.T.astype(BF16)
            dtb = _pad_lanes(_row(ssm_dt_bias[layer]))
            dtbT = ssm_dt_bias[layer].reshape(-1, 1).astype(F32)
            z, xs, bm, cm, dt, dtT = _ssd_in(h, w_in, layer, wdt, wdtT, ssm_conv_w,
                                             _row(ssm_conv_b[layer]), dtb, dtbT)
            alog = _pad_lanes(_row(ssm_a_log[layer]))
            alogT = ssm_a_log[layer].reshape(-1, 1).astype(F32)
            dskip = _row(jnp.repeat(ssm_d[layer], SSM_HEAD_DIM))
            mix_in = _ssd_scan(xs, z, bm, cm, dt, dtT, alog, alogT, dskip, _row(ssm_norm_w[layer]))
            w_proj, proj_layer = w_out, layer
        else:
            if layer == N_SSD_LAYERS:
                k, v = _kv(h, sb_w_k.astype(BF16), sb_w_v.astype(BF16))
            j = layer - N_SSD_LAYERS
            mix_in = _attn(h, w_q, j, k, v)
            w_proj, proj_layer = w_o, j
        h = _post(mix_in, h, w_proj, proj_layer, _row(ln_mix_g[layer]), _row(ln_mix_b[layer]),
                  w1, w2, layer, _row(ln_mlp_g[layer]), _row(ln_mlp_b[layer]))
    return h.reshape(1, L, D_MODEL)
```
